```python
import functools
import jax, jax.numpy as jnp
from jax import lax
import numpy as np

D_MODEL = 1024
BATCH = 2
SEQ = 8192
DEPTH = 2
DEC_BATCH = 32
DEC_SEQ = 4
PAST_LEN = 16384
PAGE_SIZE = 128

N_HEADS = 8
HEAD_DIM = 64
ATTN_DIM = N_HEADS * HEAD_DIM
CONV_DIM = D_MODEL - ATTN_DIM
CONV_GROUPS = 8
CONV_W = 3
MIX_DIM = ATTN_DIM + CONV_DIM
PROJ_DIM = 3 * ATTN_DIM + N_HEADS + 3 * CONV_DIM
F_OFF = 3 * ATTN_DIM
FORGET_BIAS_MIN = 6.0
FORGET_BIAS_MAX = 10.0
D_FF = 2816
N_EXPERTS = 8
TOP_K = 2
D_FF_EXPERT = 3584
Q_BLOCK = 128
N_DENSE = (DEPTH + 1) // 2
N_MOE = DEPTH // 2
EPS = 1e-6

kernel_name = "fox_shortconv_hymba_adaln_decoder_step"


def rmsnorm(x, g):
    xf = x.astype(jnp.float32)
    y = xf * lax.rsqrt(jnp.mean(xf * xf, axis=-1, keepdims=True) + EPS)
    return (y * g.astype(jnp.float32)).astype(x.dtype)


def group_rmsnorm(x, g, n_groups):
    shp = x.shape
    xf = x.astype(jnp.float32).reshape(*shp[:-1], n_groups, shp[-1] // n_groups)
    y = xf * lax.rsqrt(jnp.mean(xf * xf, axis=-1, keepdims=True) + EPS)
    return (y.reshape(shp) * g.astype(jnp.float32)).astype(x.dtype)


def ada_mod(c, w, b):
    m = (jax.nn.silu(c) @ w + b).reshape(c.shape[0], 1, 6, D_MODEL)
    return tuple(m[:, :, i] for i in range(6))


def in_project(h, w, b):
    bsz, t = h.shape[:2]
    p = h @ w + b
    idx = [int(i) for i in np.cumsum([ATTN_DIM] * 3 + [N_HEADS] + [CONV_DIM] * 2)]
    q, k, v, f, gb, gc, u = jnp.split(p, idx, axis=-1)
    heads = lambda a: a.reshape(bsz, t, N_HEADS, HEAD_DIM)
    logf = jax.nn.log_sigmoid(f.astype(jnp.float32))
    return heads(q), heads(k), heads(v), logf, gb, gc, u


def short_conv(u, prev, w):
    t = u.shape[1]
    up = jnp.concatenate([prev.astype(u.dtype), u], axis=1)
    y = w[0] * up[:, 0:t]
    for j in range(1, CONV_W):
        y = y + w[j] * up[:, j:j + t]
    return y, up[:, -(CONV_W - 1):]


def fox_attention_prompt(q, k, v, logf):
    bsz, t = q.shape[:2]
    scale = HEAD_DIM ** -0.5
    c = jnp.cumsum(logf, axis=1)
    cT = jnp.transpose(c, (0, 2, 1))
    kpos = jnp.arange(t)

    def block(i):
        start = i * Q_BLOCK
        qb = lax.dynamic_slice_in_dim(q, start, Q_BLOCK, axis=1)
        cb = lax.dynamic_slice_in_dim(cT, start, Q_BLOCK, axis=2)
        s = jnp.einsum('bqhd,bkhd->bhqk', qb, k).astype(jnp.float32) * scale
        s = s + cb[..., None] - cT[:, :, None, :]
        qpos = start + jnp.arange(Q_BLOCK)
        s = jnp.where(kpos[None, :] <= qpos[:, None], s, -jnp.inf)
        pr = jax.nn.softmax(s, axis=-1)
        return jnp.einsum('bhqk,bkhd->bqhd', pr.astype(v.dtype), v)

    out = lax.map(block, jnp.arange(t // Q_BLOCK))
    return jnp.moveaxis(out, 0, 1).reshape(bsz, t, N_HEADS, HEAD_DIM)


def fox_attention_sample(q, k_new, v_new, logf_new, k_past, v_past, logf_past):
    scale = HEAD_DIM ** -0.5
    td = q.shape[1]
    p_len = k_past.shape[1]
    lp = logf_past.astype(jnp.float32)
    suffix = lax.cumsum(lp, axis=1, reverse=True) - lp
    cn = jnp.cumsum(logf_new, axis=1)
    cnT = jnp.transpose(cn, (0, 2, 1))
    s_past = jnp.einsum('bqhd,bkhd->bhqk', q, k_past).astype(jnp.float32) * scale
    s_past = s_past + cnT[..., None] + jnp.transpose(suffix, (0, 2, 1))[:, :, None, :]
    s_new = jnp.einsum('bqhd,bkhd->bhqk', q, k_new).astype(jnp.float32) * scale
    s_new = s_new + cnT[..., None] - cnT[:, :, None, :]
    causal = jnp.arange(td)[None, :] <= jnp.arange(td)[:, None]
    s_new = jnp.where(causal, s_new, -jnp.inf)
    pr = jax.nn.softmax(jnp.concatenate([s_past, s_new], axis=-1), axis=-1)
    pr = pr.astype(v_new.dtype)
    return (jnp.einsum('bhqk,bkhd->bqhd', pr[..., :p_len], v_past)
            + jnp.einsum('bhqk,bkhd->bqhd', pr[..., p_len:], v_new))


def swiglu(h, wg, wu, wd):
    return (jax.nn.silu(h @ wg) * (h @ wu)) @ wd


def moe_ffn(h, router_w, router_b, wg, wu, wd):
    logits = (h @ router_w + router_b).astype(jnp.float32)
    top_v, top_i = lax.top_k(logits, TOP_K)
    gates = jax.nn.softmax(top_v, axis=-1)
    comb = jnp.sum(jax.nn.one_hot(top_i, N_EXPERTS, dtype=jnp.float32) * gates[..., None], axis=-2)
    comb = comb.astype(h.dtype)
    out = comb[..., 0:1] * swiglu(h, wg[0], wu[0], wd[0])
    for e in range(1, N_EXPERTS):
        out = out + comb[..., e:e + 1] * swiglu(h, wg[e], wu[e], wd[e])
    return out


def layer_forward(x, c, attn_fn, conv_prev, w_ada, b_ada, g1, g2, w_in, b_in,
                  conv_w, g_attn, g_conv, w_o, ffn_fn):
    bsz, t = x.shape[:2]
    sh1, sc1, gt1, sh2, sc2, gt2 = ada_mod(c, w_ada, b_ada)
    h = rmsnorm(x, g1) * (1 + sc1) + sh1
    q, k, v, logf, gb, gc, u = in_project(h, w_in, b_in)
    a = attn_fn(q, k, v, logf).reshape(bsz, t, ATTN_DIM)
    yc, conv_state = short_conv(gc * u, conv_prev, conv_w)
    m = jnp.concatenate([group_rmsnorm(a, g_attn, N_HEADS),
                         group_rmsnorm(gb * yc, g_conv, CONV_GROUPS)], axis=-1)
    x = x + gt1 * (m @ w_o)
    h2 = rmsnorm(x, g2) * (1 + sc2) + sh2
    x = x + gt2 * ffn_fn(h2)
    return x, k, v, logf, conv_state


def setup_inputs(seed: int = 0) -> dict:
    key = jax.random.key(seed)
    ks = iter(jax.random.split(key, 40))
    nrm = lambda shape, s=1.0: jax.random.normal(next(ks), shape, jnp.float32) * s
    n_pages = PAST_LEN // PAGE_SIZE
    n_used = DEC_BATCH * n_pages
    n_pool = n_used + max(1, n_used // 4)
    page_table = jax.random.permutation(next(ks), n_pool)[:n_used].reshape(DEC_BATCH, n_pages).astype(jnp.int32)
    forget_bias = jnp.linspace(FORGET_BIAS_MIN, FORGET_BIAS_MAX, N_HEADS, dtype=jnp.float32)
    b_in = nrm((DEPTH, PROJ_DIM), 0.02).at[:, F_OFF:F_OFF + N_HEADS].add(forget_bias)
    return {
        "x_prompt": nrm((BATCH, SEQ, D_MODEL)),
        "x_sample": nrm((DEC_BATCH, DEC_SEQ, D_MODEL)),
        "cache_k": nrm((DEPTH, n_pool, PAGE_SIZE, N_HEADS, HEAD_DIM)),
        "cache_v": nrm((DEPTH, n_pool, PAGE_SIZE, N_HEADS, HEAD_DIM)),
        "cache_logf": jax.nn.log_sigmoid(forget_bias + nrm((DEPTH, n_pool, PAGE_SIZE, N_HEADS), 0.5)),
        "state_conv": nrm((DEPTH, DEC_BATCH, CONV_W - 1, CONV_DIM)),
        "page_table": page_table,
        "c_prompt": nrm((BATCH, D_MODEL)),
        "c_sample": nrm((DEC_BATCH, D_MODEL)),
        "w_ada": nrm((DEPTH, D_MODEL, 6 * D_MODEL), 0.5 * D_MODEL ** -0.5),
        "b_ada": nrm((DEPTH, 6 * D_MODEL), 0.02),
        "norm1_g": 1.0 + nrm((DEPTH, D_MODEL), 0.1),
        "norm2_g": 1.0 + nrm((DEPTH, D_MODEL), 0.1),
        "w_in": nrm((DEPTH, D_MODEL, PROJ_DIM), D_MODEL ** -0.5),
        "b_in": b_in,
        "conv_w": nrm((DEPTH, CONV_W, CONV_DIM), CONV_W ** -0.5),
        "out_g_attn": 1.0 + nrm((DEPTH, ATTN_DIM), 0.1),
        "out_g_conv": 1.0 + nrm((DEPTH, CONV_DIM), 0.1),
        "w_o": nrm((DEPTH, MIX_DIM, D_MODEL), MIX_DIM ** -0.5),
        "ffn_w_gate": nrm((N_DENSE, D_MODEL, D_FF), D_MODEL ** -0.5),
        "ffn_w_up": nrm((N_DENSE, D_MODEL, D_FF), D_MODEL ** -0.5),
        "ffn_w_down": nrm((N_DENSE, D_FF, D_MODEL), D_FF ** -0.5),
        "router_w": nrm((N_MOE, D_MODEL, N_EXPERTS), D_MODEL ** -0.5),
        "router_b": nrm((N_MOE, N_EXPERTS), 0.01),
        "moe_w_gate": nrm((N_MOE, N_EXPERTS, D_MODEL, D_FF_EXPERT), D_MODEL ** -0.5),
        "moe_w_up": nrm((N_MOE, N_EXPERTS, D_MODEL, D_FF_EXPERT), D_MODEL ** -0.5),
        "moe_w_down": nrm((N_MOE, N_EXPERTS, D_FF_EXPERT, D_MODEL), D_FF_EXPERT ** -0.5),
        "final_g": 1.0 + nrm((D_MODEL,), 0.1),
    }


def reference(x_prompt, x_sample, cache_k, cache_v, cache_logf, state_conv, page_table,
              c_prompt, c_sample, w_ada, b_ada, norm1_g, norm2_g, w_in, b_in, conv_w,
              out_g_attn, out_g_conv, w_o, ffn_w_gate, ffn_w_up, ffn_w_down,
              router_w, router_b, moe_w_gate, moe_w_up, moe_w_down, final_g):
    dec_b, n_pages = page_table.shape
    past_len = n_pages * cache_k.shape[2]
    xp, xs = x_prompt, x_sample
    kp_l, vp_l, lp_l, cp_l = [], [], [], []
    ks_l, vs_l, ls_l, cs_l = [], [], [], []
    conv_zero = jnp.zeros((x_prompt.shape[0], CONV_W - 1, CONV_DIM), x_prompt.dtype)
    for l in range(DEPTH):
        j = l // 2
        if l % 2 == 0:
            ffn = functools.partial(swiglu, wg=ffn_w_gate[j], wu=ffn_w_up[j], wd=ffn_w_down[j])
        else:
            ffn = functools.partial(moe_ffn, router_w=router_w[j], router_b=router_b[j],
                                    wg=moe_w_gate[j], wu=moe_w_up[j], wd=moe_w_down[j])
        lw = (w_ada[l], b_ada[l], norm1_g[l], norm2_g[l], w_in[l], b_in[l], conv_w[l],
              out_g_attn[l], out_g_conv[l], w_o[l])
        xp, k1, v1, lf1, cs1 = layer_forward(xp, c_prompt, fox_attention_prompt, conv_zero, *lw, ffn)
        k_past = cache_k[l][page_table].reshape(dec_b, past_len, N_HEADS, HEAD_DIM)
        v_past = cache_v[l][page_table].reshape(dec_b, past_len, N_HEADS, HEAD_DIM)
        lf_past = cache_logf[l][page_table].reshape(dec_b, past_len, N_HEADS)
        attn_s = functools.partial(fox_attention_sample, k_past=k_past, v_past=v_past, logf_past=lf_past)
        xs, k2, v2, lf2, cs2 = layer_forward(xs, c_sample, attn_s, state_conv[l], *lw, ffn)
        kp_l.append(k1); vp_l.append(v1); lp_l.append(lf1); cp_l.append(cs1)
        ks_l.append(k2); vs_l.append(v2); ls_l.append(lf2); cs_l.append(cs2)
    y_prompt = rmsnorm(xp, final_g)
    y_sample = rmsnorm(xs, final_g)
    return (y_prompt, y_sample,
            jnp.stack(kp_l), jnp.stack(vp_l), jnp.stack(lp_l), jnp.stack(cp_l),
            jnp.stack(ks_l), jnp.stack(vs_l), jnp.stack(ls_l), jnp.stack(cs_l))
```

```python
import functools

import jax
import jax.numpy as jnp
import numpy as np
from jax import lax
from jax.experimental import pallas as pl
from jax.experimental.pallas import tpu as pltpu

N_HEADS = 8
HEAD_DIM = 64
CONV_GROUPS = 8
EPS = 1e-6
NEG = -1e30
LANES = 128
SUBLANES = 8
VMEM_LIMIT = 56 * 1024 * 1024

F32 = jnp.float32
BF16 = jnp.bfloat16


def _cparams(sem):
    return pltpu.CompilerParams(dimension_semantics=sem, vmem_limit_bytes=VMEM_LIMIT)


def _dot(a, b):
    return jnp.dot(a, b, preferred_element_type=F32)


def _dot_nt(a, b):
    return lax.dot_general(a, b, (((1,), (1,)), ((), ())), preferred_element_type=F32)


def _split2(x):
    hi = x.astype(BF16)
    lo = (x - hi.astype(F32)).astype(BF16)
    return hi, lo


def _split3(x):
    hi = x.astype(BF16)
    r = x - hi.astype(F32)
    mid = r.astype(BF16)
    lo = (r - mid.astype(F32)).astype(BF16)
    return hi, mid, lo


def _log_sigmoid(x):
    return jnp.minimum(x, 0.0) - jnp.log1p(jnp.exp(-jnp.abs(x)))


def _group_mean_sq(z, gmat):
    hi, lo = _split2(z * z)
    return _dot(hi, gmat) + _dot(lo, gmat)


def _group_matrix(n, group):
    idx = np.arange(n) // group
    return jnp.asarray((idx[:, None] == idx[None, :]).astype(np.float32) / group, dtype=BF16)


def _ada_kernel(c_ref, w_ref, b_ref, o_ref):
    a = jax.nn.silu(c_ref[...]).astype(BF16)
    o_ref[...] = _dot(a, w_ref[...].astype(BF16)) + b_ref[...]


def ada_modulation(c_all, w_ada, b_ada, tn=1536):
    depth, d, n6 = w_ada.shape
    m = c_all.shape[0]
    return pl.pallas_call(
        _ada_kernel,
        grid=(depth, n6 // tn),
        in_specs=[
            pl.BlockSpec((m, d), lambda l, j: (0, 0)),
            pl.BlockSpec((None, d, tn), lambda l, j: (l, 0, j)),
            pl.BlockSpec((None, 1, tn), lambda l, j: (l, 0, j)),
        ],
        out_specs=pl.BlockSpec((None, m, tn), lambda l, j: (l, 0, j)),
        out_shape=jax.ShapeDtypeStruct((depth, m, n6), F32),
        compiler_params=_cparams(("arbitrary", "arbitrary")),
        name="ada_mod",
    )(c_all, w_ada, b_ada.reshape(depth, 1, n6))


def _inproj_kernel(x_ref, mod_ref, prev_ref, g1_ref, wqkv_ref, wcv_ref, wf_ref, wft_ref,
                   bqkv_ref, bcv_ref, bf_ref, bft_ref, cw_ref, gconv_ref, gmat_ref,
                   qb_ref, kb_ref, vb_ref, k_ref, v_ref, logf_ref, negc_ref, mconv_ref, cstate_ref,
                   cbuf, ccarry, *, d, a_dim, c_dim, stride, tiles_per_group, tm, base):
    i = pl.program_id(0)
    first = (i % tiles_per_group) == 0
    x = x_ref[...]
    ms = jnp.mean(x * x, axis=-1, keepdims=True)
    y = x * lax.rsqrt(ms + EPS) * g1_ref[...]
    h = y * (1.0 + mod_ref[:, d:2 * d]) + mod_ref[:, 0:d]
    hb = h.astype(BF16)

    pq = _dot(hb, wqkv_ref[...]) + bqkv_ref[...]
    qb_ref[...] = (pq[:, 0:a_dim] * (HEAD_DIM ** -0.5)).astype(BF16)
    k = pq[:, a_dim:2 * a_dim]
    v = pq[:, 2 * a_dim:3 * a_dim]
    k_ref[...] = k
    v_ref[...] = v
    kb_ref[...] = k.astype(BF16)
    vb_ref[...] = v.astype(BF16)

    f = _dot(hb, wf_ref[...]) + bf_ref[...]
    logf_ref[...] = _log_sigmoid(f)

    ft = _dot_nt(wft_ref[...], hb) + bft_ref[...]
    c = _log_sigmoid(ft)
    lane = lax.broadcasted_iota(jnp.int32, c.shape, 1)
    shift = stride
    while shift < tm:
        c = c + jnp.where(lane >= shift, pltpu.roll(c, shift, axis=1), 0.0)
        shift *= 2

    @pl.when(first)
    def _():
        ccarry[...] = jnp.zeros_like(ccarry)

    c = c + ccarry[:, 0:1]
    negc_ref[...] = -c
    ccarry[...] = jnp.broadcast_to(c[:, tm - 1:tm], ccarry.shape)

    pc = _dot(hb, wcv_ref[...]) + bcv_ref[...]
    gb = pc[:, 0:c_dim]
    cu = pc[:, c_dim:2 * c_dim] * pc[:, 2 * c_dim:3 * c_dim]

    @pl.when(first)
    def _():
        cbuf[base - 2 * stride:base, :] = prev_ref[...]

    cbuf[base:base + tm, :] = cu
    cw = cw_ref[...]
    yc = (cw[0:1, :] * cbuf[base - 2 * stride:base - 2 * stride + tm, :]
          + cw[1:2, :] * cbuf[base - stride:base - stride + tm, :]
          + cw[2:3, :] * cu)
    tail = cbuf[base + tm - 2 * stride:base + tm, :]
    cstate_ref[...] = tail
    cbuf[base - 2 * stride:base, :] = tail

    z = gb * yc
    msq = _group_mean_sq(z, gmat_ref[...])
    mconv_ref[...] = (z * lax.rsqrt(msq + EPS) * gconv_ref[...]).astype(BF16)


def in_projection(x, mod, prev, g1, w_in, b_in, conv_w, g_conv, *, tm, stride, rows_per_group):
    n, d = x.shape
    a_dim = N_HEADS * HEAD_DIM
    c_dim = conv_w.shape[-1]
    n_tiles = n // tm
    tiles_per_group = rows_per_group // tm
    assert n % tm == 0 and rows_per_group % tm == 0
    assert stride == 1 or tiles_per_group == 1
    n_groups = n // rows_per_group
    base = -(-2 * stride // SUBLANES) * SUBLANES
    f_off = 3 * a_dim
    wqkv = w_in[:, 0:f_off].astype(BF16)
    wf = w_in[:, f_off:f_off + N_HEADS].astype(BF16)
    wcv = w_in[:, f_off + N_HEADS:].astype(BF16)
    bqkv = b_in[0:f_off].reshape(1, -1)
    bf = b_in[f_off:f_off + N_HEADS].reshape(1, -1)
    bcv = b_in[f_off + N_HEADS:].reshape(1, -1)
    gmat = _group_matrix(c_dim, c_dim // CONV_GROUPS)
    rmod = mod.shape[1]

    full = lambda shape: pl.BlockSpec(shape, lambda i: (0,) * len(shape))
    row = lambda w: pl.BlockSpec((tm, w), lambda i: (i, 0))
    grp = lambda i: i // tiles_per_group
    kern = functools.partial(_inproj_kernel, d=d, a_dim=a_dim, c_dim=c_dim, stride=stride,
                             tiles_per_group=tiles_per_group, tm=tm, base=base)
    out_shape = (
        jax.ShapeDtypeStruct((n, a_dim), BF16),
        jax.ShapeDtypeStruct((n, a_dim), BF16),
        jax.ShapeDtypeStruct((n, a_dim), BF16),
        jax.ShapeDtypeStruct((n, a_dim), F32),
        jax.ShapeDtypeStruct((n, a_dim), F32),
        jax.ShapeDtypeStruct((n, N_HEADS), F32),
        jax.ShapeDtypeStruct((n_groups, N_HEADS, rows_per_group), F32),
        jax.ShapeDtypeStruct((n, c_dim), BF16),
        jax.ShapeDtypeStruct((n_tiles, 2 * stride, c_dim), F32),
    )
    out_specs = (
        row(a_dim), row(a_dim), row(a_dim), row(a_dim), row(a_dim),
        pl.BlockSpec((tm, N_HEADS), lambda i: (i, 0)),
        pl.BlockSpec((None, N_HEADS, tm), lambda i: (grp(i), 0, i % tiles_per_group)),
        row(c_dim),
        pl.BlockSpec((None, 2 * stride, c_dim), lambda i: (i, 0, 0)),
    )
    in_specs = [
        row(d),
        pl.BlockSpec((None, rmod, mod.shape[2]), lambda i: (grp(i), 0, 0)),
        pl.BlockSpec((None, 2 * stride, c_dim), lambda i: (grp(i), 0, 0)),
        full((1, d)), full(wqkv.shape), full(wcv.shape), full(wf.shape), full((N_HEADS, d)),
        full(bqkv.shape), full(bcv.shape), full(bf.shape), full((N_HEADS, 1)),
        full(conv_w.shape), full((1, c_dim)), full(gmat.shape),
    ]
    return pl.pallas_call(
        kern,
        grid=(n_tiles,),
        in_specs=in_specs,
        out_specs=out_specs,
        out_shape=out_shape,
        scratch_shapes=[pltpu.VMEM((base + tm, c_dim), F32), pltpu.VMEM((N_HEADS, LANES), F32)],
        compiler_params=_cparams(("arbitrary",)),
        name="in_proj",
    )(x, mod, prev, g1.reshape(1, d), wqkv, wcv, wf, wf.T, bqkv, bcv, bf, bf.reshape(-1, 1),
      conv_w, g_conv.reshape(1, -1), gmat)


def _attn_kernel(qi_ref, kj_ref, q_ref, k_ref, v_ref, nc_ref, g_ref, gm_ref, o_ref,
                 m_sc, l_sc, acc_sc, *, tq, tk):
    hp = pl.program_id(1)
    p = pl.program_id(2)
    qi = qi_ref[p]
    kj = kj_ref[p]

    @pl.when(kj == 0)
    def _():
        m_sc[...] = jnp.full_like(m_sc, NEG)
        l_sc[...] = jnp.zeros_like(l_sc)
        acc_sc[...] = jnp.zeros_like(acc_sc)

    q = q_ref[...]
    k = k_ref[...]
    v = v_ref[...]
    lane = lax.broadcasted_iota(jnp.int32, (tq, LANES), 1)
    lo_half = lane < HEAD_DIM
    ahead = (lax.broadcasted_iota(jnp.int32, (tq, tk), 1)
             - lax.broadcasted_iota(jnp.int32, (tq, tk), 0))
    limit = jnp.where(kj == qi, 0, tq + tk)
    reps = tk // LANES
    alphas, pvs = [], []
    for hh in range(2):
        sel = lo_half if hh == 0 else jnp.logical_not(lo_half)
        qm = jnp.where(sel, q, jnp.zeros_like(q))
        s = _dot_nt(qm, k) + nc_ref[pl.ds(2 * hp + hh, 1), :]
        s = jnp.where(ahead > limit, NEG, s)
        m_old = m_sc[hh]
        m_new = jnp.maximum(m_old, jnp.max(s, axis=-1, keepdims=True))
        alpha = jnp.exp(m_old - m_new)
        pr = jnp.exp(s - jnp.concatenate([m_new] * reps, axis=1))
        l_sc[hh] = alpha * l_sc[hh] + jnp.sum(pr, axis=-1, keepdims=True)
        m_sc[hh] = m_new
        alphas.append(alpha)
        pvs.append(_dot(pr.astype(BF16), v))
    acc_sc[...] = (jnp.where(lo_half, alphas[0], alphas[1]) * acc_sc[...]
                   + jnp.where(lo_half, pvs[0], pvs[1]))

    @pl.when(kj == qi)
    def _():
        o = acc_sc[...] / jnp.where(lo_half, l_sc[0], l_sc[1])
        msq = _group_mean_sq(o, gm_ref[...])
        o_ref[...] = (o * lax.rsqrt(msq + EPS) * g_ref[...]).astype(BF16)


def prompt_attention(qb, kb, vb, negc, g_attn, *, batch, seq, tq):
    n, a_dim = qb.shape
    nq = seq // tq
    n_hp = a_dim // LANES
    pairs = [(i, j) for i in range(nq) for j in range(i + 1)]
    qi = jnp.asarray([p[0] for p in pairs], jnp.int32)
    kj = jnp.asarray([p[1] for p in pairs], jnp.int32)
    gm = _group_matrix(LANES, HEAD_DIM)
    kern = functools.partial(_attn_kernel, tq=tq, tk=tq)
    grid_spec = pltpu.PrefetchScalarGridSpec(
        num_scalar_prefetch=2,
        grid=(batch, n_hp, len(pairs)),
        in_specs=[
            pl.BlockSpec((tq, LANES), lambda b, h, p, qi, kj: (b * nq + qi[p], h)),
            pl.BlockSpec((tq, LANES), lambda b, h, p, qi, kj: (b * nq + kj[p], h)),
            pl.BlockSpec((tq, LANES), lambda b, h, p, qi, kj: (b * nq + kj[p], h)),
            pl.BlockSpec((None, N_HEADS, tq), lambda b, h, p, qi, kj: (b, 0, kj[p])),
            pl.BlockSpec((1, LANES), lambda b, h, p, qi, kj: (0, h)),
            pl.BlockSpec((LANES, LANES), lambda b, h, p, qi, kj: (0, 0)),
        ],
        out_specs=pl.BlockSpec((tq, LANES), lambda b, h, p, qi, kj: (b * nq + qi[p], h)),
        scratch_shapes=[pltpu.VMEM((2, tq, LANES), F32), pltpu.VMEM((2, tq, LANES), F32),
                        pltpu.VMEM((tq, LANES), F32)],
    )
    return pl.pallas_call(
        kern,
        grid_spec=grid_spec,
        out_shape=jax.ShapeDtypeStruct((n, a_dim), BF16),
        compiler_params=_cparams(("arbitrary", "arbitrary", "arbitrary")),
        name="prompt_attn",
    )(qi, kj, qb, kb, vb, negc, g_attn.reshape(1, -1), gm)


def _suffix_kernel(x_ref, m_ref, o_ref):
    hi, mid, lo = _split3(x_ref[...])
    m = m_ref[...]
    o_ref[...] = _dot(hi, m) + _dot(mid, m) + _dot(lo, m)


def page_suffix_sums(cache_logf, tp=512):
    depth, n_pool, page, nh = cache_logf.shape
    w = page * nh
    rows = depth * n_pool
    src = np.arange(w)
    dst = np.arange(w)
    sel = ((src[:, None] % nh) == (dst[None, :] // page)) & ((src[:, None] // nh) >= (dst[None, :] % page))
    mat = jnp.asarray(sel.astype(np.float32), dtype=BF16)
    while rows % tp:
        tp //= 2
    out = pl.pallas_call(
        _suffix_kernel,
        grid=(rows // tp,),
        in_specs=[pl.BlockSpec((tp, w), lambda i: (i, 0)), pl.BlockSpec((w, w), lambda i: (0, 0))],
        out_specs=pl.BlockSpec((tp, w), lambda i: (i, 0)),
        out_shape=jax.ShapeDtypeStruct((rows, w), F32),
        compiler_params=_cparams(("arbitrary",)),
        name="page_suffix",
    )(cache_logf.reshape(rows, w), mat)
    return out.reshape(rows, nh, page)


def _sample_attn_kernel(pt_ref, q_ref, kn_ref, vn_ref, bn_ref, g_ref, gm_ref, hm_ref, *rest,
                        ppc, n_chunks, page, n_steps):
    k_refs = rest[0:ppc]
    v_refs = rest[ppc:2 * ppc]
    w_refs = rest[2 * ppc:3 * ppc]
    o_ref = rest[3 * ppc]
    m_sc, l_sc, acc_sc, tail_sc, row_sc = rest[3 * ppc + 1:]
    c = pl.program_id(1)
    q = q_ref[...]
    a_dim = q.shape[-1]
    wide = a_dim // LANES

    @pl.when(c == 0)
    def _():
        s = _dot_nt(q, kn_ref[...]) + bn_ref[...]
        m = jnp.max(s, axis=-1, keepdims=True)
        pr = jnp.exp(s - m)
        m_sc[...] = jnp.broadcast_to(m, m_sc.shape)
        l_sc[...] = jnp.broadcast_to(jnp.sum(pr, axis=-1, keepdims=True), l_sc.shape)
        acc_sc[...] = _dot(pr.astype(BF16), vn_ref[...])
        tail_sc[...] = jnp.zeros_like(tail_sc)

    lane = lax.broadcasted_iota(jnp.int32, (N_HEADS, page), 1)
    tail = tail_sc[...]
    s_parts = [None] * ppc
    for r in reversed(range(ppc)):
        wi = w_refs[r][...]
        wex = jnp.where(lane < page - 1, pltpu.roll(wi, page - 1, axis=1), 0.0)
        bias = jnp.concatenate([wex] * n_steps, axis=0) + tail
        s_parts[r] = _dot_nt(q, k_refs[r][...].astype(BF16)) + bias
        tot = jnp.broadcast_to(wi[:, 0:1], wi.shape)
        tail = tail + jnp.concatenate([tot] * n_steps, axis=0)
    tail_sc[...] = tail

    m_old = m_sc[...]
    m_cur = s_parts[0]
    for r in range(1, ppc):
        m_cur = jnp.maximum(m_cur, s_parts[r])
    m_new = jnp.maximum(m_old, jnp.max(m_cur, axis=-1, keepdims=True))
    alpha = jnp.exp(m_old - m_new)
    l_new = alpha * l_sc[...]
    pv = None
    for r in range(ppc):
        pr = jnp.exp(s_parts[r] - m_new)
        l_new = l_new + jnp.sum(pr, axis=-1, keepdims=True)
        t = _dot(pr.astype(BF16), v_refs[r][...].astype(BF16))
        pv = t if pv is None else pv + t
    m_sc[...] = m_new
    l_sc[...] = l_new
    acc_sc[...] = jnp.concatenate([alpha] * wide, axis=1) * acc_sc[...] + pv

    @pl.when(c == n_chunks - 1)
    def _():
        o = acc_sc[...] / jnp.concatenate([l_sc[...]] * wide, axis=1)
        row_sc[...] = jnp.zeros_like(row_sc)
        for t in range(n_steps):
            blk = o[t * N_HEADS:(t + 1) * N_HEADS, :] * hm_ref[...]
            row_sc[t:t + 1, :] = jnp.sum(blk, axis=0, keepdims=True)
        a = row_sc[...]
        msq = _group_mean_sq(a, gm_ref[...])
        res = a * lax.rsqrt(msq + EPS) * g_ref[...]
        o_ref[...] = res[0:n_steps, :].astype(BF16)


def sample_attention(pt_flat, qbig, knew, vnew, bias_new, g_attn, cache_k2, cache_v2, wsfx,
                     *, n_pages, ppc=8):
    bd, rows, a_dim = qbig.shape
    n_steps = rows // N_HEADS
    page = cache_k2.shape[1]
    n_chunks = n_pages // ppc
    assert n_pages % ppc == 0 and n_steps <= SUBLANES
    gm = _group_matrix(a_dim, HEAD_DIM)
    hm = jnp.asarray((np.arange(a_dim)[None, :] // HEAD_DIM == np.arange(N_HEADS)[:, None]).astype(np.float32))

    def page_map(r):
        return lambda b, c, pt: (pt[b * n_pages + (n_chunks - 1 - c) * ppc + r], 0, 0)

    per_b = lambda shape: pl.BlockSpec((None,) + shape, lambda b, c, pt: (b, 0, 0))
    const = lambda shape: pl.BlockSpec(shape, lambda b, c, pt: (0, 0))
    in_specs = [per_b((rows, a_dim)), per_b(knew.shape[1:]), per_b(vnew.shape[1:]), per_b(bias_new.shape[1:]),
                const((1, a_dim)), const(gm.shape), const(hm.shape)]
    in_specs += [pl.BlockSpec((None, page, a_dim), page_map(r)) for r in range(ppc)]
    in_specs += [pl.BlockSpec((None, page, a_dim), page_map(r)) for r in range(ppc)]
    in_specs += [pl.BlockSpec((None, N_HEADS, page), page_map(r)) for r in range(ppc)]
    kern = functools.partial(_sample_attn_kernel, ppc=ppc, n_chunks=n_chunks, page=page, n_steps=n_steps)
    grid_spec = pltpu.PrefetchScalarGridSpec(
        num_scalar_prefetch=1,
        grid=(bd, n_chunks),
        in_specs=in_specs,
        out_specs=pl.BlockSpec((None, n_steps, a_dim), lambda b, c, pt: (b, 0, 0)),
        scratch_shapes=[pltpu.VMEM((rows, LANES), F32), pltpu.VMEM((rows, LANES), F32),
                        pltpu.VMEM((rows, a_dim), F32), pltpu.VMEM((rows, LANES), F32),
                        pltpu.VMEM((SUBLANES, a_dim), F32)],
    )
    return pl.pallas_call(
        kern,
        grid_spec=grid_spec,
        out_shape=jax.ShapeDtypeStruct((bd, n_steps, a_dim), BF16),
        compiler_params=_cparams(("arbitrary", "arbitrary")),
        name="sample_attn",
    )(pt_flat, qbig, knew, vnew, bias_new, g_attn.reshape(1, -1), gm, hm,
      *([cache_k2] * ppc), *([cache_v2] * ppc), *([wsfx] * ppc))


def _outproj_kernel(ma_ref, mc_ref, x_ref, mod_ref, woa_ref, woc_ref, g2_ref, *rest, d, route):
    if route:
        rwcat_ref, rwhi_ref, rb_ref, x1_ref, h2_ref, comb_ref = rest
    else:
        x1_ref, h2_ref = rest
    o = _dot(ma_ref[...], woa_ref[...]) + _dot(mc_ref[...], woc_ref[...])
    x1 = x_ref[...] + mod_ref[:, 2 * d:3 * d] * o
    x1_ref[...] = x1
    ms = jnp.mean(x1 * x1, axis=-1, keepdims=True)
    y = x1 * lax.rsqrt(ms + EPS) * g2_ref[...]
    h2 = y * (1.0 + mod_ref[:, 4 * d:5 * d]) + mod_ref[:, 3 * d:4 * d]
    h2_ref[...] = h2.astype(BF16)
    if route:
        h_hi, h_lo = _split2(h2)
        both = _dot(h_hi, rwcat_ref[...])
        lg = both[:, 0:LANES] + both[:, LANES:2 * LANES] + _dot(h_lo, rwhi_ref[...]) + rb_ref[...]
        lane = lax.broadcasted_iota(jnp.int32, lg.shape, 1)
        m1 = jnp.max(lg, axis=-1, keepdims=True)
        i1 = jnp.min(jnp.where(lg == m1, lane, LANES), axis=-1, keepdims=True)
        lg2 = jnp.where(lane == i1, NEG, lg)
        m2 = jnp.max(lg2, axis=-1, keepdims=True)
        i2 = jnp.min(jnp.where(lg2 == m2, lane, LANES), axis=-1, keepdims=True)
        e2 = jnp.exp(m2 - m1)
        den = 1.0 + e2
        comb_ref[...] = jnp.where(lane == i1, 1.0 / den, 0.0) + jnp.where(lane == i2, e2 / den, 0.0)


def out_projection(m_attn, m_conv, x, mod, w_o, g2, router, *, tm, rows_per_group):
    n, d = x.shape
    a_dim = m_attn.shape[1]
    woa = w_o[0:a_dim].astype(BF16)
    woc = w_o[a_dim:].astype(BF16)
    tiles_per_group = rows_per_group // tm
    rmod = mod.shape[1]
    full = lambda shape: pl.BlockSpec(shape, lambda i: (0,) * len(shape))
    row = lambda w: pl.BlockSpec((tm, w), lambda i: (i, 0))
    in_specs = [row(a_dim), row(m_conv.shape[1]), row(d),
                pl.BlockSpec((None, rmod, mod.shape[2]), lambda i: (i // tiles_per_group, 0, 0)),
                full(woa.shape), full(woc.shape), full((1, d))]
    args = [m_attn, m_conv, x, mod, woa, woc, g2.reshape(1, d)]
    out_shape = [jax.ShapeDtypeStruct((n, d), F32), jax.ShapeDtypeStruct((n, d), BF16)]
    out_specs = [row(d), row(d)]
    if router is not None:
        rw, rb = router
        ne = rw.shape[1]
        rw_hi, rw_lo = _split2(jnp.pad(rw, ((0, 0), (0, LANES - ne))))
        rwcat = jnp.concatenate([rw_hi, rw_lo], axis=1)
        rbp = jnp.concatenate([rb, jnp.full((LANES - ne,), NEG, F32)]).reshape(1, LANES)
        in_specs += [full(rwcat.shape), full(rw_hi.shape), full((1, LANES))]
        args += [rwcat, rw_hi, rbp]
        out_shape.append(jax.ShapeDtypeStruct((n, LANES), F32))
        out_specs.append(row(LANES))
    kern = functools.partial(_outproj_kernel, d=d, route=router is not None)
    return pl.pallas_call(
        kern,
        grid=(n // tm,),
        in_specs=in_specs,
        out_specs=out_specs,
        out_shape=out_shape,
        compiler_params=_cparams(("arbitrary",)),
        name="out_proj",
    )(*args)


def _ffn_kernel(h_ref, x_ref, mod_ref, comb_ref, wg_ref, wu_ref, wd_ref, fg_ref, o_ref, acc,
                *, d, n_e, n_f, weighted, final_norm):
    e = pl.program_id(1)
    f = pl.program_id(2)

    @pl.when(jnp.logical_and(e == 0, f == 0))
    def _():
        acc[...] = jnp.zeros_like(acc)

    h = h_ref[...]
    g = _dot(h, wg_ref[...].astype(BF16))
    u = _dot(h, wu_ref[...].astype(BF16))
    a = (jax.nn.silu(g) * u).astype(BF16)
    y = _dot(a, wd_ref[...].astype(BF16))
    if weighted:
        comb = comb_ref[...]
        lane = lax.broadcasted_iota(jnp.int32, comb.shape, 1)
        y = jnp.sum(jnp.where(lane == e, comb, 0.0), axis=-1, keepdims=True) * y
    acc[...] += y

    @pl.when(jnp.logical_and(e == n_e - 1, f == n_f - 1))
    def _():
        x2 = x_ref[...] + mod_ref[:, 5 * d:6 * d] * acc[...]
        if final_norm:
            ms = jnp.mean(x2 * x2, axis=-1, keepdims=True)
            x2 = x2 * lax.rsqrt(ms + EPS) * fg_ref[...]
        o_ref[...] = x2


def ffn_block(h2, x1, mod, comb, wg, wu, wd, final_g, *, tm, tf, rows_per_group, final_norm):
    n, d = x1.shape
    n_e, _, ff = wg.shape
    n_f = ff // tf
    assert ff % tf == 0 and n % tm == 0
    weighted = comb is not None
    if comb is None:
        comb = jnp.ones((n, LANES), F32)
    tiles_per_group = rows_per_group // tm
    rmod = mod.shape[1]
    kern = functools.partial(_ffn_kernel, d=d, n_e=n_e, n_f=n_f, weighted=weighted, final_norm=final_norm)
    return pl.pallas_call(
        kern,
        grid=(n // tm, n_e, n_f),
        in_specs=[
            pl.BlockSpec((tm, d), lambda i, e, f: (i, 0)),
            pl.BlockSpec((tm, d), lambda i, e, f: (i, 0)),
            pl.BlockSpec((None, rmod, mod.shape[2]), lambda i, e, f: (i // tiles_per_group, 0, 0)),
            pl.BlockSpec((tm, LANES), lambda i, e, f: (i, 0)),
            pl.BlockSpec((None, d, tf), lambda i, e, f: (e, 0, f)),
            pl.BlockSpec((None, d, tf), lambda i, e, f: (e, 0, f)),
            pl.BlockSpec((None, tf, d), lambda i, e, f: (e, f, 0)),
            pl.BlockSpec((1, d), lambda i, e, f: (0, 0)),
        ],
        out_specs=pl.BlockSpec((tm, d), lambda i, e, f: (i, 0)),
        out_shape=jax.ShapeDtypeStruct((n, d), F32),
        scratch_shapes=[pltpu.VMEM((tm, d), F32)],
        compiler_params=_cparams(("arbitrary", "arbitrary", "arbitrary")),
        name="ffn",
    )(h2, x1, mod, comb, wg, wu, wd, final_g.reshape(1, d))


def _pick_tf(ff, target):
    best = LANES
    for t in range(LANES, target + 1, LANES):
        if ff % t == 0:
            best = t
    return best


def kernel(x_prompt, x_sample, cache_k, cache_v, cache_logf, state_conv, page_table, c_prompt, c_sample, w_ada, b_ada, norm1_g, norm2_g, w_in, b_in, conv_w, out_g_attn, out_g_conv, w_o, ffn_w_gate, ffn_w_up, ffn_w_down, router_w, router_b, moe_w_gate, moe_w_up, moe_w_down, final_g):
    batch, seq, d = x_prompt.shape
    bd, ts, _ = x_sample.shape
    depth, n_pool, page, nh, dh = cache_k.shape
    assert (nh, dh) == (N_HEADS, HEAD_DIM)
    a_dim = nh * dh
    c_dim = conv_w.shape[-1]
    n_pages = page_table.shape[1]
    np_rows = batch * seq
    ns_rows = bd * ts

    tm_p = min(512, seq)
    tq = min(512, seq)
    tm_f = min(1024, seq)

    n_seq = batch + bd
    pad = -n_seq % SUBLANES
    c_all = jnp.concatenate([c_prompt, c_sample, jnp.zeros((pad, d), F32)], axis=0)
    mods = ada_modulation(c_all, w_ada, b_ada)

    xp = x_prompt.reshape(np_rows, d)
    xs = jnp.transpose(x_sample, (1, 0, 2)).reshape(ns_rows, d)
    cache_k2 = cache_k.reshape(depth * n_pool, page, a_dim)
    cache_v2 = cache_v.reshape(depth * n_pool, page, a_dim)
    wsfx = page_suffix_sums(cache_logf)
    conv_zero = jnp.zeros((batch, 2, c_dim), F32)
    head_of_lane = np.arange(a_dim) // HEAD_DIM
    row_head = np.arange(ts * N_HEADS) % N_HEADS
    qmask = jnp.asarray(row_head[:, None] == head_of_lane[None, :])
    causal = jnp.asarray(np.arange(SUBLANES)[None, :] <= (np.arange(ts * N_HEADS) // N_HEADS)[:, None])

    outs = {k: [] for k in ("kp", "vp", "lp", "cp", "ks", "vs", "ls", "cs")}
    for l in range(depth):
        j = l // 2
        last = l == depth - 1
        mod_p = mods[l, 0:batch][:, None, :]
        mod_s = jnp.tile(mods[l, batch:batch + bd], (ts, 1))[None]
        if l % 2 == 0:
            wg, wu, wd = ffn_w_gate[j][None], ffn_w_up[j][None], ffn_w_down[j][None]
            router = None
        else:
            wg, wu, wd = moe_w_gate[j], moe_w_up[j], moe_w_down[j]
            router = (router_w[j], router_b[j])
        tf = _pick_tf(wg.shape[-1], 512)

        qb, kb, vb, k1, v1, lf1, negc, mconv, cst = in_projection(
            xp, mod_p, conv_zero, norm1_g[l], w_in[l], b_in[l], conv_w[l], out_g_conv[l],
            tm=tm_p, stride=1, rows_per_group=seq)
        m_attn = prompt_attention(qb, kb, vb, negc, out_g_attn[l], batch=batch, seq=seq, tq=tq)
        res = out_projection(m_attn, mconv, xp, mod_p, w_o[l], norm2_g[l], router,
                             tm=tm_p, rows_per_group=seq)
        comb = res[2] if router is not None else None
        xp = ffn_block(res[1], res[0], mod_p, comb, wg, wu, wd, final_g,
                       tm=tm_f, tf=tf, rows_per_group=seq, final_norm=last)
        outs["kp"].append(k1.reshape(batch, seq, nh, dh))
        outs["vp"].append(v1.reshape(batch, seq, nh, dh))
        outs["lp"].append(lf1.reshape(batch, seq, nh))
        tiles_per_seq = seq // tm_p
        outs["cp"].append(cst[tiles_per_seq - 1::tiles_per_seq])

        prev_s = jnp.transpose(state_conv[l], (1, 0, 2)).reshape(1, 2 * bd, c_dim)
        qb, kb, vb, k2, v2, lf2, negc, mconv, cst = in_projection(
            xs, mod_s, prev_s, norm1_g[l], w_in[l], b_in[l], conv_w[l], out_g_conv[l],
            tm=ns_rows, stride=bd, rows_per_group=ns_rows)
        to_seq = lambda a: jnp.transpose(a.reshape(ts, bd, -1), (1, 0, 2))
        q_seq = to_seq(qb)
        qbig = jnp.where(qmask[None], jnp.repeat(q_seq, N_HEADS, axis=1), jnp.zeros((), BF16))
        rows_pad = ((0, 0), (0, SUBLANES - ts), (0, 0))
        knew = jnp.pad(to_seq(kb), rows_pad)
        vnew = jnp.pad(to_seq(vb), rows_pad)
        nc = jnp.transpose(negc[0].reshape(nh, ts, bd), (2, 1, 0))
        nc = jnp.pad(jnp.transpose(nc, (0, 2, 1)), ((0, 0), (0, 0), (0, SUBLANES - ts)))
        bias_new = jnp.where(causal[None], jnp.tile(nc, (1, ts, 1)), NEG)
        pt_flat = (page_table + l * n_pool).reshape(-1).astype(jnp.int32)
        a_s = sample_attention(pt_flat, qbig, knew, vnew, bias_new, out_g_attn[l],
                               cache_k2, cache_v2, wsfx, n_pages=n_pages)
        m_attn_s = jnp.transpose(a_s, (1, 0, 2)).reshape(ns_rows, a_dim)
        res = out_projection(m_attn_s, mconv, xs, mod_s, w_o[l], norm2_g[l], router,
                             tm=ns_rows, rows_per_group=ns_rows)
        comb = res[2] if router is not None else None
        xs = ffn_block(res[1], res[0], mod_s, comb, wg, wu, wd, final_g,
                       tm=ns_rows, tf=tf, rows_per_group=ns_rows, final_norm=last)
        outs["ks"].append(to_seq(k2).reshape(bd, ts, nh, dh))
        outs["vs"].append(to_seq(v2).reshape(bd, ts, nh, dh))
        outs["ls"].append(to_seq(lf2))
        outs["cs"].append(jnp.transpose(cst.reshape(2, bd, c_dim), (1, 0, 2)))

    y_prompt = xp.reshape(batch, seq, d)
    y_sample = jnp.transpose(xs.reshape(ts, bd, d), (1, 0, 2))
    st = lambda key: jnp.stack(outs[key])
    return (y_prompt, y_sample, st("kp"), st("vp"), st("lp"), st("cp"),
            st("ks"), st("vs"), st("ls"), st("cs"))
```

```python
import functools

import jax
import jax.numpy as jnp
import numpy as np
from jax import lax
from jax.experimental import pallas as pl
from jax.experimental.pallas import tpu as pltpu

N_HEADS = 8
HEAD_DIM = 64
CONV_GROUPS = 8
EPS = 1e-6
NEG = -1e30
LOG2E = 1.4426950408889634
LANES = 128
SUBLANES = 8
VMEM_LIMIT = 56 * 1024 * 1024

F32 = jnp.float32
BF16 = jnp.bfloat16


def _cparams(sem):
    return pltpu.CompilerParams(dimension_semantics=sem, vmem_limit_bytes=VMEM_LIMIT)


def _dot(a, b):
    return jnp.dot(a, b, preferred_element_type=F32)


def _dot_nt(a, b):
    return lax.dot_general(a, b, (((1,), (1,)), ((), ())), preferred_element_type=F32)


def _split2(x):
    hi = x.astype(BF16)
    lo = (x - hi.astype(F32)).astype(BF16)
    return hi, lo


def _log_sigmoid(x):
    return jnp.minimum(x, 0.0) - jnp.log1p(jnp.exp(-jnp.abs(x)))


def _group_mean_sq(z, gmat):
    hi, lo = _split2(z * z)
    return _dot(hi, gmat) + _dot(lo, gmat)


def _drop_inputs(kern, start, count, *refs):
    return kern(*refs[:start], *refs[start + count:])


def _group_matrix(n, group):
    idx = np.arange(n) // group
    return jnp.asarray((idx[:, None] == idx[None, :]).astype(np.float32) / group, dtype=BF16)


def _ada_kernel(c_ref, w_ref, b_ref, o_ref):
    a = jax.nn.silu(c_ref[...]).astype(BF16)
    o_ref[...] = _dot(a, w_ref[...].astype(BF16)) + b_ref[...]


def ada_modulation(c_all, w_ada, b_ada, tn=1536):
    depth, d, n6 = w_ada.shape
    m = c_all.shape[0]
    return pl.pallas_call(
        _ada_kernel,
        grid=(depth, n6 // tn),
        in_specs=[
            pl.BlockSpec((m, d), lambda l, j: (0, 0)),
            pl.BlockSpec((None, d, tn), lambda l, j: (l, 0, j)),
            pl.BlockSpec((None, 1, tn), lambda l, j: (l, 0, j)),
        ],
        out_specs=pl.BlockSpec((None, m, tn), lambda l, j: (l, 0, j)),
        out_shape=jax.ShapeDtypeStruct((depth, m, n6), F32),
        compiler_params=_cparams(("arbitrary", "arbitrary")),
        name="ada_mod",
    )(c_all, w_ada, b_ada.reshape(depth, 1, n6))


def _inproj_kernel(x_ref, mod_ref, prev_ref, g1_ref, wqkv_ref, wcv_ref, wf_ref, wft_ref,
                   bqkv_ref, bcv_ref, bf_ref, bft_ref, cw_ref, gconv_ref, gmat_ref,
                   qb_ref, kb_ref, vb_ref, k_ref, v_ref, logf_ref, negc_ref, mconv_ref, cstate_ref,
                   cbuf, ccarry, *, d, a_dim, c_dim, stride, tiles_per_group, tm, base, logit_scale):
    i = pl.program_id(0)
    first = (i % tiles_per_group) == 0
    x = x_ref[...]
    ms = jnp.mean(x * x, axis=-1, keepdims=True)
    y = x * lax.rsqrt(ms + EPS) * g1_ref[...]
    h = y * (1.0 + mod_ref[:, d:2 * d]) + mod_ref[:, 0:d]
    hb = h.astype(BF16)

    pq = _dot(hb, wqkv_ref[...]) + bqkv_ref[...]
    qb_ref[...] = (pq[:, 0:a_dim] * (HEAD_DIM ** -0.5 * logit_scale)).astype(BF16)
    k = pq[:, a_dim:2 * a_dim]
    v = pq[:, 2 * a_dim:3 * a_dim]
    for hd in range(N_HEADS):
        k_ref[pl.ds(hd, tm, stride=N_HEADS), :] = k[:, hd * HEAD_DIM:(hd + 1) * HEAD_DIM]
        v_ref[pl.ds(hd, tm, stride=N_HEADS), :] = v[:, hd * HEAD_DIM:(hd + 1) * HEAD_DIM]
    kb_ref[...] = k.astype(BF16)
    vb_ref[...] = v.astype(BF16)

    f = _dot(hb, wf_ref[...]) + bf_ref[...]
    logf_ref[...] = _log_sigmoid(f)

    ft = _dot_nt(wft_ref[...], hb) + bft_ref[...]
    c = _log_sigmoid(ft)
    lane = lax.broadcasted_iota(jnp.int32, c.shape, 1)
    shift = stride
    while shift < tm:
        c = c + jnp.where(lane >= shift, pltpu.roll(c, shift, axis=1), 0.0)
        shift *= 2

    @pl.when(first)
    def _():
        ccarry[...] = jnp.zeros_like(ccarry)

    c = c + ccarry[:, 0:1]
    negc_ref[...] = c * (-logit_scale)
    ccarry[...] = jnp.broadcast_to(c[:, tm - 1:tm], ccarry.shape)

    pc = _dot(hb, wcv_ref[...]) + bcv_ref[...]
    gb = pc[:, 0:c_dim]
    cu = pc[:, c_dim:2 * c_dim] * pc[:, 2 * c_dim:3 * c_dim]

    @pl.when(first)
    def _():
        cbuf[base - 2 * stride:base, :] = prev_ref[...]

    cbuf[base:base + tm, :] = cu
    cw = cw_ref[...]
    yc = (cw[0:1, :] * cbuf[base - 2 * stride:base - 2 * stride + tm, :]
          + cw[1:2, :] * cbuf[base - stride:base - stride + tm, :]
          + cw[2:3, :] * cu)
    tail = cbuf[base + tm - 2 * stride:base + tm, :]
    cstate_ref[...] = tail
    cbuf[base - 2 * stride:base, :] = tail

    z = gb * yc
    msq = _group_mean_sq(z, gmat_ref[...])
    mconv_ref[...] = (z * lax.rsqrt(msq + EPS) * gconv_ref[...]).astype(BF16)


def in_projection(x, mod, prev, g1, w_in, b_in, conv_w, g_conv, *, tm, stride, rows_per_group,
                  logit_scale=1.0, layer=0, depth=1, kv_buf=None):
    n, d = x.shape
    a_dim = N_HEADS * HEAD_DIM
    c_dim = conv_w.shape[-1]
    n_tiles = n // tm
    tiles_per_group = rows_per_group // tm
    assert n % tm == 0 and rows_per_group % tm == 0
    assert stride == 1 or tiles_per_group == 1
    n_groups = n // rows_per_group
    base = -(-2 * stride // SUBLANES) * SUBLANES
    f_off = 3 * a_dim
    wqkv = w_in[:, 0:f_off].astype(BF16)
    wf = w_in[:, f_off:f_off + N_HEADS].astype(BF16)
    wcv = w_in[:, f_off + N_HEADS:].astype(BF16)
    bqkv = b_in[0:f_off].reshape(1, -1)
    bf = b_in[f_off:f_off + N_HEADS].reshape(1, -1)
    bcv = b_in[f_off + N_HEADS:].reshape(1, -1)
    gmat = _group_matrix(c_dim, c_dim // CONV_GROUPS)
    rmod = mod.shape[1]

    full = lambda shape: pl.BlockSpec(shape, lambda i: (0,) * len(shape))
    row = lambda w: pl.BlockSpec((tm, w), lambda i: (i, 0))
    grp = lambda i: i // tiles_per_group
    kern = functools.partial(_inproj_kernel, d=d, a_dim=a_dim, c_dim=c_dim, stride=stride,
                             tiles_per_group=tiles_per_group, tm=tm, base=base, logit_scale=logit_scale)
    out_shape = (
        jax.ShapeDtypeStruct((n, a_dim), BF16),
        jax.ShapeDtypeStruct((n, a_dim), BF16),
        jax.ShapeDtypeStruct((n, a_dim), BF16),
        jax.ShapeDtypeStruct((depth * n * N_HEADS, HEAD_DIM), F32),
        jax.ShapeDtypeStruct((depth * n * N_HEADS, HEAD_DIM), F32),
        jax.ShapeDtypeStruct((n, N_HEADS), F32),
        jax.ShapeDtypeStruct((n_groups, N_HEADS, rows_per_group), F32),
        jax.ShapeDtypeStruct((n, c_dim), BF16),
        jax.ShapeDtypeStruct((n_tiles, 2 * stride, c_dim), F32),
    )
    kv_rows = pl.BlockSpec((tm * N_HEADS, HEAD_DIM), lambda i: (layer * n_tiles + i, 0))
    out_specs = (
        row(a_dim), row(a_dim), row(a_dim), kv_rows, kv_rows,
        pl.BlockSpec((tm, N_HEADS), lambda i: (i, 0)),
        pl.BlockSpec((None, N_HEADS, tm), lambda i: (grp(i), 0, i % tiles_per_group)),
        row(c_dim),
        pl.BlockSpec((None, 2 * stride, c_dim), lambda i: (i, 0, 0)),
    )
    in_specs = [
        row(d),
        pl.BlockSpec((None, rmod, mod.shape[2]), lambda i: (grp(i), 0, 0)),
        pl.BlockSpec((None, 2 * stride, c_dim), lambda i: (grp(i), 0, 0)),
        full((1, d)), full(wqkv.shape), full(wcv.shape), full(wf.shape), full((N_HEADS, d)),
        full(bqkv.shape), full(bcv.shape), full(bf.shape), full((N_HEADS, 1)),
        full(conv_w.shape), full((1, c_dim)), full(gmat.shape),
    ]
    args = [x, mod, prev, g1.reshape(1, d), wqkv, wcv, wf, wf.T, bqkv, bcv, bf, bf.reshape(-1, 1),
            conv_w, g_conv.reshape(1, -1), gmat]
    aliases = {}
    if kv_buf is not None:
        aliases = {len(args): 3, len(args) + 1: 4}
        in_specs += [pl.BlockSpec(memory_space=pl.ANY)] * 2
        args += list(kv_buf)
        kern = functools.partial(_drop_inputs, kern, len(args) - 2, 2)
    return pl.pallas_call(
        kern,
        grid=(n_tiles,),
        in_specs=in_specs,
        out_specs=out_specs,
        out_shape=out_shape,
        input_output_aliases=aliases,
        scratch_shapes=[pltpu.VMEM((base + tm, c_dim), F32), pltpu.VMEM((N_HEADS, LANES), F32)],
        compiler_params=_cparams(("arbitrary",)),
        name="in_proj",
    )(*args)


def _attn_kernel(qi_ref, kj_ref, q_ref, k_ref, v_ref, nc_ref, g_ref, gm_ref, o_ref,
                 m_sc, l_sc, acc_sc, *, tq, tk):
    hp = pl.program_id(1)
    p = pl.program_id(2)
    qi = qi_ref[p]
    kj = kj_ref[p]

    @pl.when(kj == 0)
    def _():
        m_sc[...] = jnp.full_like(m_sc, NEG)
        l_sc[...] = jnp.zeros_like(l_sc)
        acc_sc[...] = jnp.zeros_like(acc_sc)

    lo_half = lax.broadcasted_iota(jnp.int32, (tq, LANES), 1) < HEAD_DIM
    lo_half_k = lax.broadcasted_iota(jnp.int32, (tk, LANES), 1) < HEAD_DIM
    reps = tk // LANES

    def step(masked):
        q = q_ref[...]
        k = k_ref[...]
        v = v_ref[...]
        one = jnp.ones_like(v)
        alphas, pvs = [], []
        for hh in range(2):
            mine = lo_half if hh == 0 else jnp.logical_not(lo_half)
            mine_k = lo_half_k if hh == 0 else jnp.logical_not(lo_half_k)
            qm = jnp.where(mine, q, jnp.zeros_like(q))
            s = _dot_nt(qm, k) + nc_ref[pl.ds(2 * hp + hh, 1), :]
            if masked:
                s = jnp.where(lax.broadcasted_iota(jnp.int32, (tq, tk), 1)
                              > lax.broadcasted_iota(jnp.int32, (tq, tk), 0), NEG, s)
            m_old = m_sc[hh]
            m_new = jnp.maximum(m_old, jnp.max(s, axis=-1, keepdims=True))
            alphas.append(jnp.exp2(m_old - m_new))
            pr = jnp.exp2(s - jnp.concatenate([m_new] * reps, axis=1))
            m_sc[hh] = m_new
            pvs.append(_dot(pr.astype(BF16), jnp.where(mine_k, v, one)))
        acc_sc[...] = (jnp.where(lo_half, alphas[0], alphas[1]) * acc_sc[...]
                       + jnp.where(lo_half, pvs[0], pvs[1]))
        l_sc[...] = (jnp.where(lo_half, alphas[1], alphas[0]) * l_sc[...]
                     + jnp.where(lo_half, pvs[1], pvs[0]))

    @pl.when(kj != qi)
    def _():
        step(False)

    @pl.when(kj == qi)
    def _():
        step(True)
        o = acc_sc[...] / pltpu.roll(l_sc[...], HEAD_DIM, axis=1)
        msq = _group_mean_sq(o, gm_ref[...])
        o_ref[...] = (o * lax.rsqrt(msq + EPS) * g_ref[...]).astype(BF16)


def prompt_attention(qb, kb, vb, negc, g_attn, *, batch, seq, tq):
    n, a_dim = qb.shape
    nq = seq // tq
    n_hp = a_dim // LANES
    pairs = [(i, j) for i in range(nq) for j in range(i + 1)]
    qi = jnp.asarray([p[0] for p in pairs], jnp.int32)
    kj = jnp.asarray([p[1] for p in pairs], jnp.int32)
    gm = _group_matrix(LANES, HEAD_DIM)
    kern = functools.partial(_attn_kernel, tq=tq, tk=tq)
    grid_spec = pltpu.PrefetchScalarGridSpec(
        num_scalar_prefetch=2,
        grid=(batch, n_hp, len(pairs)),
        in_specs=[
            pl.BlockSpec((tq, LANES), lambda b, h, p, qi, kj: (b * nq + qi[p], h)),
            pl.BlockSpec((tq, LANES), lambda b, h, p, qi, kj: (b * nq + kj[p], h)),
            pl.BlockSpec((tq, LANES), lambda b, h, p, qi, kj: (b * nq + kj[p], h)),
            pl.BlockSpec((None, N_HEADS, tq), lambda b, h, p, qi, kj: (b, 0, kj[p])),
            pl.BlockSpec((1, LANES), lambda b, h, p, qi, kj: (0, h)),
            pl.BlockSpec((LANES, LANES), lambda b, h, p, qi, kj: (0, 0)),
        ],
        out_specs=pl.BlockSpec((tq, LANES), lambda b, h, p, qi, kj: (b * nq + qi[p], h)),
        scratch_shapes=[pltpu.VMEM((2, tq, LANES), F32), pltpu.VMEM((tq, LANES), F32),
                        pltpu.VMEM((tq, LANES), F32)],
    )
    return pl.pallas_call(
        kern,
        grid_spec=grid_spec,
        out_shape=jax.ShapeDtypeStruct((n, a_dim), BF16),
        compiler_params=_cparams(("arbitrary", "arbitrary", "arbitrary")),
        name="prompt_attn",
    )(qi, kj, qb, kb, vb, negc, g_attn.reshape(1, -1), gm)


def _suffix_kernel(x_ref, o_ref, *, nh, w):
    x = x_ref[...]
    lane = lax.broadcasted_iota(jnp.int32, x.shape, 1)
    c = x
    t = x
    shift = nh
    while shift < w:
        c = c + jnp.where(lane + shift < w, pltpu.roll(c, w - shift, axis=1), 0.0)
        t = t + pltpu.roll(t, shift, axis=1)
        shift *= 2
    o_ref[:, 0:w] = c - x
    o_ref[:, w:2 * w] = t


def page_suffix_sums(cache_logf, tp=512):
    depth, n_pool, page, nh = cache_logf.shape
    w = page * nh
    rows = depth * n_pool
    while rows % tp:
        tp //= 2
    return pl.pallas_call(
        functools.partial(_suffix_kernel, nh=nh, w=w),
        grid=(rows // tp,),
        in_specs=[pl.BlockSpec((tp, w), lambda i: (i, 0))],
        out_specs=pl.BlockSpec((tp, 2 * w), lambda i: (i, 0)),
        out_shape=jax.ShapeDtypeStruct((rows, 2 * w), F32),
        compiler_params=_cparams(("arbitrary",)),
        name="page_suffix",
    )(cache_logf.reshape(rows, w))


def _sample_attn_kernel(pt_ref, q_ref, kn_ref, vn_ref, bn_ref, mb_ref, g_ref, *rest,
                        ppc, n_chunks, n_pages, w):
    k_refs = rest[0:ppc]
    v_refs = rest[ppc:2 * ppc]
    s_refs = rest[2 * ppc:3 * ppc]
    o_ref = rest[3 * ppc]
    m_sc, l_sc, acc_sc, tail_sc = rest[3 * ppc + 1:]
    b = pl.program_id(0)
    c = pl.program_id(1)
    q = q_ref[...]
    dh = q.shape[-1]

    @pl.when(c == 0)
    def _():
        s = _dot_nt(q, kn_ref[...]) + bn_ref[...]
        m = jnp.max(s, axis=-1, keepdims=True)
        pr = jnp.exp(s - m)
        m_sc[...] = jnp.broadcast_to(m, m_sc.shape)
        l_sc[...] = jnp.broadcast_to(jnp.sum(pr, axis=-1, keepdims=True), l_sc.shape)
        acc_sc[...] = _dot(pr.astype(BF16), vn_ref[...])
        tail_sc[...] = jnp.zeros_like(tail_sc)

    mb = mb_ref[...]
    tail = tail_sc[...]
    first_page = b * n_pages + (n_chunks - 1 - c) * ppc
    s_parts = [None] * ppc
    for r in reversed(range(ppc)):
        row = pt_ref[first_page + r] % SUBLANES
        sfx = s_refs[r][pl.ds(row, 1), :]
        kb = k_refs[r][...].reshape(w, dh).astype(BF16)
        s_parts[r] = _dot_nt(q, kb) + (mb + (sfx[:, 0:w] + tail))
        tail = tail + sfx[:, w:2 * w]
    tail_sc[...] = tail

    m_old = m_sc[...]
    m_cur = s_parts[0]
    for r in range(1, ppc):
        m_cur = jnp.maximum(m_cur, s_parts[r])
    m_new = jnp.maximum(m_old, jnp.max(m_cur, axis=-1, keepdims=True))
    alpha = jnp.exp(m_old - m_new)
    m_wide = jnp.concatenate([m_new] * (w // LANES), axis=1)
    l_new = alpha * l_sc[...]
    pv = None
    for r in range(ppc):
        pr = jnp.exp(s_parts[r] - m_wide)
        l_new = l_new + jnp.sum(pr, axis=-1, keepdims=True)
        t = _dot(pr.astype(BF16), v_refs[r][...].reshape(w, dh).astype(BF16))
        pv = t if pv is None else pv + t
    m_sc[...] = m_new
    l_sc[...] = l_new
    acc_sc[...] = alpha[:, 0:dh] * acc_sc[...] + pv

    @pl.when(c == n_chunks - 1)
    def _():
        o = acc_sc[...] / l_sc[:, 0:dh]
        msq = jnp.mean(o * o, axis=-1, keepdims=True)
        o_ref[...] = (o * lax.rsqrt(msq + EPS) * g_ref[...]).astype(BF16)


def sample_attention(pt_flat, q2, knew, vnew, bias_new, g_rows, cache_k, cache_v, sfx, layer,
                     *, n_pages, ppc=8):
    bd, rows, dh = q2.shape
    depth, n_pool, page, nh, _ = cache_k.shape
    w = page * nh
    n_chunks = n_pages // ppc
    assert n_pages % ppc == 0 and w % LANES == 0 and (depth * n_pool) % SUBLANES == 0
    col_head = np.arange(w) % nh
    row_head = np.arange(rows) % nh
    mb = jnp.asarray(np.where(row_head[:, None] == col_head[None, :], 0.0, NEG).astype(np.float32))

    def page_idx(b, c, pt, r):
        return pt[b * n_pages + (n_chunks - 1 - c) * ppc + r]

    def kv_map(r):
        return lambda b, c, pt: (layer, page_idx(b, c, pt, r), 0, 0, 0)

    def sfx_map(r):
        return lambda b, c, pt: ((layer * n_pool + page_idx(b, c, pt, r)) // SUBLANES, 0)

    per_b = lambda shape: pl.BlockSpec((None,) + shape, lambda b, c, pt: (b, 0, 0))
    const = lambda shape: pl.BlockSpec(shape, lambda b, c, pt: (0, 0))
    in_specs = [per_b((rows, dh)), per_b(knew.shape[1:]), per_b(vnew.shape[1:]), per_b(bias_new.shape[1:]),
                const(mb.shape), const((rows, dh))]
    in_specs += [pl.BlockSpec((None, None, page, nh, dh), kv_map(r)) for r in range(ppc)]
    in_specs += [pl.BlockSpec((None, None, page, nh, dh), kv_map(r)) for r in range(ppc)]
    in_specs += [pl.BlockSpec((SUBLANES, 2 * w), sfx_map(r)) for r in range(ppc)]
    kern = functools.partial(_sample_attn_kernel, ppc=ppc, n_chunks=n_chunks, n_pages=n_pages, w=w)
    grid_spec = pltpu.PrefetchScalarGridSpec(
        num_scalar_prefetch=1,
        grid=(bd, n_chunks),
        in_specs=in_specs,
        out_specs=pl.BlockSpec((None, rows, dh), lambda b, c, pt: (b, 0, 0)),
        scratch_shapes=[pltpu.VMEM((rows, LANES), F32), pltpu.VMEM((rows, LANES), F32),
                        pltpu.VMEM((rows, dh), F32), pltpu.VMEM((1, w), F32)],
    )
    return pl.pallas_call(
        kern,
        grid_spec=grid_spec,
        out_shape=jax.ShapeDtypeStruct((bd, rows, dh), BF16),
        compiler_params=_cparams(("arbitrary", "arbitrary")),
        name="sample_attn",
    )(pt_flat, q2, knew, vnew, bias_new, mb, g_rows,
      *([cache_k] * ppc), *([cache_v] * ppc), *([sfx] * ppc))


def _outproj_kernel(ma_ref, mc_ref, x_ref, mod_ref, woa_ref, woc_ref, g2_ref, *rest, d, route):
    if route:
        rwcat_ref, rwhi_ref, rb_ref, x1_ref, h2_ref, comb_ref = rest
    else:
        x1_ref, h2_ref = rest
    o = _dot(ma_ref[...], woa_ref[...]) + _dot(mc_ref[...], woc_ref[...])
    x1 = x_ref[...] + mod_ref[:, 2 * d:3 * d] * o
    x1_ref[...] = x1
    ms = jnp.mean(x1 * x1, axis=-1, keepdims=True)
    y = x1 * lax.rsqrt(ms + EPS) * g2_ref[...]
    h2 = y * (1.0 + mod_ref[:, 4 * d:5 * d]) + mod_ref[:, 3 * d:4 * d]
    h2_ref[...] = h2.astype(BF16)
    if route:
        h_hi, h_lo = _split2(h2)
        both = _dot(h_hi, rwcat_ref[...])
        lg = both[:, 0:LANES] + both[:, LANES:2 * LANES] + _dot(h_lo, rwhi_ref[...]) + rb_ref[...]
        lane = lax.broadcasted_iota(jnp.int32, lg.shape, 1)
        m1 = jnp.max(lg, axis=-1, keepdims=True)
        i1 = jnp.min(jnp.where(lg == m1, lane, LANES), axis=-1, keepdims=True)
        lg2 = jnp.where(lane == i1, NEG, lg)
        m2 = jnp.max(lg2, axis=-1, keepdims=True)
        i2 = jnp.min(jnp.where(lg2 == m2, lane, LANES), axis=-1, keepdims=True)
        e2 = jnp.exp(m2 - m1)
        den = 1.0 + e2
        comb_ref[...] = jnp.where(lane == i1, 1.0 / den, 0.0) + jnp.where(lane == i2, e2 / den, 0.0)


def out_projection(m_attn, m_conv, x, mod, w_o, g2, router, *, tm, rows_per_group):
    n, d = x.shape
    a_dim = m_attn.shape[1]
    woa = w_o[0:a_dim].astype(BF16)
    woc = w_o[a_dim:].astype(BF16)
    tiles_per_group = rows_per_group // tm
    rmod = mod.shape[1]
    full = lambda shape: pl.BlockSpec(shape, lambda i: (0,) * len(shape))
    row = lambda w: pl.BlockSpec((tm, w), lambda i: (i, 0))
    in_specs = [row(a_dim), row(m_conv.shape[1]), row(d),
                pl.BlockSpec((None, rmod, mod.shape[2]), lambda i: (i // tiles_per_group, 0, 0)),
                full(woa.shape), full(woc.shape), full((1, d))]
    args = [m_attn, m_conv, x, mod, woa, woc, g2.reshape(1, d)]
    out_shape = [jax.ShapeDtypeStruct((n, d), F32), jax.ShapeDtypeStruct((n, d), BF16)]
    out_specs = [row(d), row(d)]
    if router is not None:
        rw, rb = router
        ne = rw.shape[1]
        rw_hi, rw_lo = _split2(jnp.pad(rw, ((0, 0), (0, LANES - ne))))
        rwcat = jnp.concatenate([rw_hi, rw_lo], axis=1)
        rbp = jnp.concatenate([rb, jnp.full((LANES - ne,), NEG, F32)]).reshape(1, LANES)
        in_specs += [full(rwcat.shape), full(rw_hi.shape), full((1, LANES))]
        args += [rwcat, rw_hi, rbp]
        out_shape.append(jax.ShapeDtypeStruct((n, LANES), F32))
        out_specs.append(row(LANES))
    kern = functools.partial(_outproj_kernel, d=d, route=router is not None)
    return pl.pallas_call(
        kern,
        grid=(n // tm,),
        in_specs=in_specs,
        out_specs=out_specs,
        out_shape=out_shape,
        compiler_params=_cparams(("arbitrary",)),
        name="out_proj",
    )(*args)


def _residual_out(x_ref, mod_ref, acc, fg_ref, o_ref, d, final_norm):
    x2 = x_ref[...] + mod_ref[:, 5 * d:6 * d] * acc[...]
    if final_norm:
        ms = jnp.mean(x2 * x2, axis=-1, keepdims=True)
        x2 = x2 * lax.rsqrt(ms + EPS) * fg_ref[...]
    o_ref[...] = x2


def _ffn_kernel(h_ref, x_ref, mod_ref, wg_ref, wu_ref, wd_ref, fg_ref, o_ref, acc,
                *, d, n_f, final_norm):
    f = pl.program_id(1)

    @pl.when(f == 0)
    def _():
        acc[...] = jnp.zeros_like(acc)

    h = h_ref[...]
    g = _dot(h, wg_ref[...].astype(BF16))
    u = _dot(h, wu_ref[...].astype(BF16))
    a = (jax.nn.silu(g) * u).astype(BF16)
    acc[...] += _dot(a, wd_ref[...].astype(BF16))

    @pl.when(f == n_f - 1)
    def _():
        _residual_out(x_ref, mod_ref, acc, fg_ref, o_ref, d, final_norm)


def ffn_block(h2, x1, mod, wg, wu, wd, final_g, *, tm, tf, rows_per_group, final_norm):
    n, d = x1.shape
    ff = wg.shape[-1]
    n_f = ff // tf
    assert ff % tf == 0 and n % tm == 0
    tiles_per_group = rows_per_group // tm
    rmod = mod.shape[1]
    kern = functools.partial(_ffn_kernel, d=d, n_f=n_f, final_norm=final_norm)
    return pl.pallas_call(
        kern,
        grid=(n // tm, n_f),
        in_specs=[
            pl.BlockSpec((tm, d), lambda i, f: (i, 0)),
            pl.BlockSpec((tm, d), lambda i, f: (i, 0)),
            pl.BlockSpec((None, rmod, mod.shape[2]), lambda i, f: (i // tiles_per_group, 0, 0)),
            pl.BlockSpec((d, tf), lambda i, f: (0, f)),
            pl.BlockSpec((d, tf), lambda i, f: (0, f)),
            pl.BlockSpec((tf, d), lambda i, f: (f, 0)),
            pl.BlockSpec((1, d), lambda i, f: (0, 0)),
        ],
        out_specs=pl.BlockSpec((tm, d), lambda i, f: (i, 0)),
        out_shape=jax.ShapeDtypeStruct((n, d), F32),
        scratch_shapes=[pltpu.VMEM((tm, d), F32)],
        compiler_params=_cparams(("arbitrary", "arbitrary")),
        name="ffn",
    )(h2, x1, mod, wg, wu, wd, final_g.reshape(1, d))


def _moe_kernel(h_ref, x_ref, mod_ref, comb_ref, tri_ref, wg_ref, wu_ref, wd_ref, fg_ref, o_ref,
                acc, rkc, rkr, xc, yc, nch_ref, *, d, n_e, n_f, tm, ch, final_norm):
    e = pl.program_id(1)
    f = pl.program_id(2)

    @pl.when(jnp.logical_and(e == 0, f == 0))
    def _():
        acc[...] = jnp.zeros_like(acc)
        sel = comb_ref[...] > 0.0
        rank = _dot(tri_ref[...], jnp.where(sel, 1.0, 0.0).astype(BF16))
        rk = jnp.where(sel, rank, -1.0)
        rkc[...] = rk
        rkr[...] = rk.T

    @pl.when(f == 0)
    def _():
        rk_row = rkr[pl.ds(e, 1), :]
        cnt = jnp.sum(jnp.where(rk_row >= 0.0, 1.0, 0.0)).astype(jnp.int32)
        nch = (cnt + (ch - 1)) // ch
        nch_ref[0] = nch
        slot = lax.broadcasted_iota(jnp.int32, (ch, tm), 0).astype(F32)

        def pack(c, carry):
            r0 = pl.multiple_of(c * ch, ch)
            onehot = jnp.where(rk_row - (c * ch).astype(F32) == slot, 1.0, 0.0).astype(BF16)
            xc[pl.ds(r0, ch), :] = _dot(onehot, h_ref[...]).astype(BF16)
            yc[pl.ds(r0, ch), :] = jnp.zeros((ch, d), F32)
            return carry

        lax.fori_loop(0, nch, pack, 0)

    nch = nch_ref[0]

    def expert(c, carry):
        r0 = pl.multiple_of(c * ch, ch)
        xb = xc[pl.ds(r0, ch), :]
        g = _dot(xb, wg_ref[...])
        u = _dot(xb, wu_ref[...])
        a = (jax.nn.silu(g) * u).astype(BF16)
        yc[pl.ds(r0, ch), :] += _dot(a, wd_ref[...])
        return carry

    lax.fori_loop(0, nch, expert, 0)

    @pl.when(f == n_f - 1)
    def _():
        lane = lax.broadcasted_iota(jnp.int32, (tm, LANES), 1)
        pick = lane == e
        gate = jnp.sum(jnp.where(pick, comb_ref[...], 0.0), axis=-1, keepdims=True)
        rk_col = jnp.sum(jnp.where(pick, rkc[...], 0.0), axis=-1, keepdims=True)
        slot = lax.broadcasted_iota(jnp.int32, (tm, ch), 1).astype(F32)

        def spread(c, carry):
            r0 = pl.multiple_of(c * ch, ch)
            onehot = jnp.where(rk_col - (c * ch).astype(F32) == slot, 1.0, 0.0).astype(BF16)
            acc[...] += gate * _dot(onehot, yc[pl.ds(r0, ch), :].astype(BF16))
            return carry

        lax.fori_loop(0, nch, spread, 0)

    @pl.when(jnp.logical_and(e == n_e - 1, f == n_f - 1))
    def _():
        _residual_out(x_ref, mod_ref, acc, fg_ref, o_ref, d, final_norm)


def moe_block(h2, x1, mod, comb, wg, wu, wd, final_g, *, tm, tf, ch, rows_per_group, final_norm):
    n, d = x1.shape
    n_e, _, ff = wg.shape
    n_f = ff // tf
    assert ff % tf == 0 and n % tm == 0 and ch % LANES == 0
    cap = -(-tm // ch) * ch
    tiles_per_group = rows_per_group // tm
    rmod = mod.shape[1]
    tri = jnp.asarray(np.tril(np.ones((tm, tm), np.float32), -1), dtype=BF16)
    kern = functools.partial(_moe_kernel, d=d, n_e=n_e, n_f=n_f, tm=tm, ch=ch, final_norm=final_norm)
    return pl.pallas_call(
        kern,
        grid=(n // tm, n_e, n_f),
        in_specs=[
            pl.BlockSpec((tm, d), lambda i, e, f: (i, 0)),
            pl.BlockSpec((tm, d), lambda i, e, f: (i, 0)),
            pl.BlockSpec((None, rmod, mod.shape[2]), lambda i, e, f: (i // tiles_per_group, 0, 0)),
            pl.BlockSpec((tm, LANES), lambda i, e, f: (i, 0)),
            pl.BlockSpec((tm, tm), lambda i, e, f: (0, 0)),
            pl.BlockSpec((None, d, tf), lambda i, e, f: (e, 0, f)),
            pl.BlockSpec((None, d, tf), lambda i, e, f: (e, 0, f)),
            pl.BlockSpec((None, tf, d), lambda i, e, f: (e, f, 0)),
            pl.BlockSpec((1, d), lambda i, e, f: (0, 0)),
        ],
        out_specs=pl.BlockSpec((tm, d), lambda i, e, f: (i, 0)),
        out_shape=jax.ShapeDtypeStruct((n, d), F32),
        scratch_shapes=[pltpu.VMEM((tm, d), F32), pltpu.VMEM((tm, LANES), F32), pltpu.VMEM((LANES, tm), F32),
                        pltpu.VMEM((cap, d), BF16), pltpu.VMEM((cap, d), F32), pltpu.SMEM((1,), jnp.int32)],
        compiler_params=_cparams(("arbitrary", "arbitrary", "arbitrary")),
        name="moe",
    )(h2, x1, mod, comb, tri, wg, wu, wd, final_g.reshape(1, d))


def _pick_tf(ff, target):
    best = LANES
    for t in range(LANES, target + 1, LANES):
        if ff % t == 0:
            best = t
    return best


def kernel(x_prompt, x_sample, cache_k, cache_v, cache_logf, state_conv, page_table, c_prompt, c_sample, w_ada, b_ada, norm1_g, norm2_g, w_in, b_in, conv_w, out_g_attn, out_g_conv, w_o, ffn_w_gate, ffn_w_up, ffn_w_down, router_w, router_b, moe_w_gate, moe_w_up, moe_w_down, final_g):
    batch, seq, d = x_prompt.shape
    bd, ts, _ = x_sample.shape
    depth, n_pool, page, nh, dh = cache_k.shape
    assert (nh, dh) == (N_HEADS, HEAD_DIM)
    a_dim = nh * dh
    c_dim = conv_w.shape[-1]
    n_pages = page_table.shape[1]
    np_rows = batch * seq
    ns_rows = bd * ts
    rows_q = ts * nh

    tm_p = min(512, seq)
    tq = min(512, seq)
    tm_f = min(1024, seq)
    ch_p = 384 if tm_f >= 1024 else LANES

    n_seq = batch + bd
    pad = -n_seq % SUBLANES
    c_all = jnp.concatenate([c_prompt, c_sample, jnp.zeros((pad, d), F32)], axis=0)
    mods = ada_modulation(c_all, w_ada, b_ada)

    xp = x_prompt.reshape(np_rows, d)
    xs = jnp.transpose(x_sample, (1, 0, 2)).reshape(ns_rows, d)
    sfx = page_suffix_sums(cache_logf)
    conv_zero = jnp.zeros((batch, 2, c_dim), F32)
    rq = np.arange(rows_q)
    new_ok = jnp.asarray((rq[:, None] % nh == rq[None, :] % nh) & (rq[None, :] // nh <= rq[:, None] // nh))

    outs = {k: [] for k in ("lp", "cp", "ls", "cs")}
    kv_p = (jnp.zeros((depth * np_rows * nh, dh), F32),) * 2
    kv_s = (jnp.zeros((depth * ns_rows * nh, dh), F32),) * 2
    for l in range(depth):
        j = l // 2
        last = l == depth - 1
        mod_p = mods[l, 0:batch][:, None, :]
        mod_s = jnp.tile(mods[l, batch:batch + bd], (ts, 1))[None]
        moe = l % 2 == 1
        if moe:
            wg, wu, wd = moe_w_gate[j].astype(BF16), moe_w_up[j].astype(BF16), moe_w_down[j].astype(BF16)
            router = (router_w[j], router_b[j])
        else:
            wg, wu, wd = ffn_w_gate[j], ffn_w_up[j], ffn_w_down[j]
            router = None
        tf = _pick_tf(wg.shape[-1], 512)

        def ffn(res, mod, tm, ch, rows_per_group):
            if moe:
                return moe_block(res[1], res[0], mod, res[2], wg, wu, wd, final_g, tm=tm, tf=tf, ch=ch,
                                 rows_per_group=rows_per_group, final_norm=last)
            return ffn_block(res[1], res[0], mod, wg, wu, wd, final_g, tm=tm, tf=tf,
                             rows_per_group=rows_per_group, final_norm=last)

        qb, kb, vb, k1, v1, lf1, negc, mconv, cst = in_projection(
            xp, mod_p, conv_zero, norm1_g[l], w_in[l], b_in[l], conv_w[l], out_g_conv[l],
            tm=tm_p, stride=1, rows_per_group=seq, logit_scale=LOG2E, layer=l, depth=depth, kv_buf=kv_p)
        kv_p = (k1, v1)
        m_attn = prompt_attention(qb, kb, vb, negc, out_g_attn[l], batch=batch, seq=seq, tq=tq)
        res = out_projection(m_attn, mconv, xp, mod_p, w_o[l], norm2_g[l], router,
                             tm=tm_p, rows_per_group=seq)
        xp = ffn(res, mod_p, tm_f, ch_p, seq)
        outs["lp"].append(lf1.reshape(batch, seq, nh))
        tiles_per_seq = seq // tm_p
        outs["cp"].append(cst[tiles_per_seq - 1::tiles_per_seq])

        prev_s = jnp.transpose(state_conv[l], (1, 0, 2)).reshape(1, 2 * bd, c_dim)
        qb, kb, vb, k2, v2, lf2, negc, mconv, cst = in_projection(
            xs, mod_s, prev_s, norm1_g[l], w_in[l], b_in[l], conv_w[l], out_g_conv[l],
            tm=ns_rows, stride=bd, rows_per_group=ns_rows, layer=l, depth=depth, kv_buf=kv_s)
        kv_s = (k2, v2)
        to_seq = lambda a: jnp.transpose(a.reshape(ts, bd, -1), (1, 0, 2))
        per_head = lambda a: to_seq(a).reshape(bd, rows_q, dh)
        nc = jnp.transpose(negc[0].reshape(nh, ts, bd), (2, 1, 0)).reshape(bd, 1, rows_q)
        bias_new = jnp.where(new_ok[None], nc, NEG)
        g_rows = jnp.tile(out_g_attn[l].reshape(nh, dh), (ts, 1))
        pt_flat = page_table.reshape(-1).astype(jnp.int32)
        a_s = sample_attention(pt_flat, per_head(qb), per_head(kb), per_head(vb), bias_new, g_rows,
                               cache_k, cache_v, sfx, l, n_pages=n_pages)
        m_attn_s = jnp.transpose(a_s.reshape(bd, ts, a_dim), (1, 0, 2)).reshape(ns_rows, a_dim)
        res = out_projection(m_attn_s, mconv, xs, mod_s, w_o[l], norm2_g[l], router,
                             tm=ns_rows, rows_per_group=ns_rows)
        xs = ffn(res, mod_s, ns_rows, LANES, ns_rows)
        outs["ls"].append(to_seq(lf2))
        outs["cs"].append(jnp.transpose(cst.reshape(2, bd, c_dim), (1, 0, 2)))

    y_prompt = xp.reshape(batch, seq, d)
    y_sample = jnp.transpose(xs.reshape(ts, bd, d), (1, 0, 2))
    st = lambda key: jnp.stack(outs[key])
    kv_prompt = lambda a: a.reshape(depth, batch, seq, nh, dh)
    kv_sample = lambda a: jnp.transpose(a.reshape(depth, ts, bd, nh, dh), (0, 2, 1, 3, 4))
    return (y_prompt, y_sample, kv_prompt(kv_p[0]), kv_prompt(kv_p[1]), st("lp"), st("cp"),
            kv_sample(kv_s[0]), kv_sample(kv_s[1]), st("ls"), st("cs"))
```

```python
import functools

import jax
import jax.numpy as jnp
import numpy as np
from jax import lax
from jax.experimental import pallas as pl
from jax.experimental.pallas import tpu as pltpu

N_HEADS = 8
HEAD_DIM = 64
CONV_GROUPS = 8
EPS = 1e-6
NEG = -1e30
LOG2E = 1.4426950408889634
LANES = 128
SUBLANES = 8
VMEM_LIMIT = 56 * 1024 * 1024

F32 = jnp.float32
BF16 = jnp.bfloat16


def _cparams(sem):
    return pltpu.CompilerParams(dimension_semantics=sem, vmem_limit_bytes=VMEM_LIMIT)


def _dot(a, b):
    return jnp.dot(a, b, preferred_element_type=F32)


def _dot_nt(a, b):
    return lax.dot_general(a, b, (((1,), (1,)), ((), ())), preferred_element_type=F32)


def _split2(x):
    hi = x.astype(BF16)
    lo = (x - hi.astype(F32)).astype(BF16)
    return hi, lo


def _log_sigmoid(x):
    return jnp.minimum(x, 0.0) - jnp.log1p(jnp.exp(-jnp.abs(x)))


def _group_mean_sq(z, gmat):
    hi, lo = _split2(z * z)
    return _dot(hi, gmat) + _dot(lo, gmat)


def _drop_inputs(kern, start, count, *refs):
    return kern(*refs[:start], *refs[start + count:])


def _group_matrix(n, group):
    idx = np.arange(n) // group
    return jnp.asarray((idx[:, None] == idx[None, :]).astype(np.float32) / group, dtype=BF16)


def _ada_kernel(c_ref, w_ref, b_ref, o_ref):
    a = jax.nn.silu(c_ref[...]).astype(BF16)
    o_ref[...] = _dot(a, w_ref[...].astype(BF16)) + b_ref[...]


def ada_modulation(c_all, w_ada, b_ada, tn=1536):
    depth, d, n6 = w_ada.shape
    m = c_all.shape[0]
    return pl.pallas_call(
        _ada_kernel,
        grid=(depth, n6 // tn),
        in_specs=[
            pl.BlockSpec((m, d), lambda l, j: (0, 0)),
            pl.BlockSpec((None, d, tn), lambda l, j: (l, 0, j)),
            pl.BlockSpec((None, 1, tn), lambda l, j: (l, 0, j)),
        ],
        out_specs=pl.BlockSpec((None, m, tn), lambda l, j: (l, 0, j)),
        out_shape=jax.ShapeDtypeStruct((depth, m, n6), F32),
        compiler_params=_cparams(("arbitrary", "arbitrary")),
        name="ada_mod",
    )(c_all, w_ada, b_ada.reshape(depth, 1, n6))


def _inproj_kernel(x_ref, mod_ref, prev_ref, g1_ref, wqkv_ref, wcv_ref, wf_ref, wft_ref,
                   bqkv_ref, bcv_ref, bf_ref, bft_ref, cw_ref, gconv_ref, gmat_ref,
                   qb_ref, kb_ref, vb_ref, k_ref, v_ref, logf_ref, negc_ref, mconv_ref, cstate_ref,
                   cbuf, ccarry, *, d, a_dim, c_dim, stride, tiles_per_group, tm, base, logit_scale):
    i = pl.program_id(0)
    first = (i % tiles_per_group) == 0
    x = x_ref[...]
    ms = jnp.mean(x * x, axis=-1, keepdims=True)
    y = x * lax.rsqrt(ms + EPS) * g1_ref[...]
    h = y * (1.0 + mod_ref[:, d:2 * d]) + mod_ref[:, 0:d]
    hb = h.astype(BF16)

    pq = _dot(hb, wqkv_ref[...]) + bqkv_ref[...]
    qb_ref[...] = (pq[:, 0:a_dim] * (HEAD_DIM ** -0.5 * logit_scale)).astype(BF16)
    k = pq[:, a_dim:2 * a_dim]
    v = pq[:, 2 * a_dim:3 * a_dim]
    for hd in range(N_HEADS):
        k_ref[pl.ds(hd, tm, stride=N_HEADS), :] = k[:, hd * HEAD_DIM:(hd + 1) * HEAD_DIM]
        v_ref[pl.ds(hd, tm, stride=N_HEADS), :] = v[:, hd * HEAD_DIM:(hd + 1) * HEAD_DIM]
    kb_ref[...] = k.astype(BF16)
    vb_ref[...] = v.astype(BF16)

    f = _dot(hb, wf_ref[...]) + bf_ref[...]
    logf_ref[...] = _log_sigmoid(f)

    ft = _dot_nt(wft_ref[...], hb) + bft_ref[...]
    c = _log_sigmoid(ft)
    lane = lax.broadcasted_iota(jnp.int32, c.shape, 1)
    shift = stride
    while shift < tm:
        c = c + jnp.where(lane >= shift, pltpu.roll(c, shift, axis=1), 0.0)
        shift *= 2

    @pl.when(first)
    def _():
        ccarry[...] = jnp.zeros_like(ccarry)

    c = c + ccarry[:, 0:1]
    negc_ref[...] = c * (-logit_scale)
    ccarry[...] = jnp.broadcast_to(c[:, tm - 1:tm], ccarry.shape)

    pc = _dot(hb, wcv_ref[...]) + bcv_ref[...]
    gb = pc[:, 0:c_dim]
    cu = pc[:, c_dim:2 * c_dim] * pc[:, 2 * c_dim:3 * c_dim]

    @pl.when(first)
    def _():
        cbuf[base - 2 * stride:base, :] = prev_ref[...]

    cbuf[base:base + tm, :] = cu
    cw = cw_ref[...]
    yc = (cw[0:1, :] * cbuf[base - 2 * stride:base - 2 * stride + tm, :]
          + cw[1:2, :] * cbuf[base - stride:base - stride + tm, :]
          + cw[2:3, :] * cu)
    tail = cbuf[base + tm - 2 * stride:base + tm, :]
    cstate_ref[...] = tail
    cbuf[base - 2 * stride:base, :] = tail

    z = gb * yc
    msq = _group_mean_sq(z, gmat_ref[...])
    mconv_ref[...] = (z * lax.rsqrt(msq + EPS) * gconv_ref[...]).astype(BF16)


def in_projection(x, mod, prev, g1, w_in, b_in, conv_w, g_conv, *, tm, stride, rows_per_group,
                  logit_scale=1.0, layer=0, depth=1, kv_buf=None):
    n, d = x.shape
    a_dim = N_HEADS * HEAD_DIM
    c_dim = conv_w.shape[-1]
    n_tiles = n // tm
    tiles_per_group = rows_per_group // tm
    assert n % tm == 0 and rows_per_group % tm == 0
    assert stride == 1 or tiles_per_group == 1
    n_groups = n // rows_per_group
    base = -(-2 * stride // SUBLANES) * SUBLANES
    f_off = 3 * a_dim
    wqkv = w_in[:, 0:f_off].astype(BF16)
    wf = w_in[:, f_off:f_off + N_HEADS].astype(BF16)
    wcv = w_in[:, f_off + N_HEADS:].astype(BF16)
    bqkv = b_in[0:f_off].reshape(1, -1)
    bf = b_in[f_off:f_off + N_HEADS].reshape(1, -1)
    bcv = b_in[f_off + N_HEADS:].reshape(1, -1)
    gmat = _group_matrix(c_dim, c_dim // CONV_GROUPS)
    rmod = mod.shape[1]

    full = lambda shape: pl.BlockSpec(shape, lambda i: (0,) * len(shape))
    row = lambda w: pl.BlockSpec((tm, w), lambda i: (i, 0))
    grp = lambda i: i // tiles_per_group
    kern = functools.partial(_inproj_kernel, d=d, a_dim=a_dim, c_dim=c_dim, stride=stride,
                             tiles_per_group=tiles_per_group, tm=tm, base=base, logit_scale=logit_scale)
    out_shape = (
        jax.ShapeDtypeStruct((n, a_dim), BF16),
        jax.ShapeDtypeStruct((n, a_dim), BF16),
        jax.ShapeDtypeStruct((n, a_dim), BF16),
        jax.ShapeDtypeStruct((depth * n * N_HEADS, HEAD_DIM), F32),
        jax.ShapeDtypeStruct((depth * n * N_HEADS, HEAD_DIM), F32),
        jax.ShapeDtypeStruct((n, N_HEADS), F32),
        jax.ShapeDtypeStruct((n_groups, N_HEADS, rows_per_group), F32),
        jax.ShapeDtypeStruct((n, c_dim), BF16),
        jax.ShapeDtypeStruct((n_tiles, 2 * stride, c_dim), F32),
    )
    kv_rows = pl.BlockSpec((tm * N_HEADS, HEAD_DIM), lambda i: (layer * n_tiles + i, 0))
    out_specs = (
        row(a_dim), row(a_dim), row(a_dim), kv_rows, kv_rows,
        pl.BlockSpec((tm, N_HEADS), lambda i: (i, 0)),
        pl.BlockSpec((None, N_HEADS, tm), lambda i: (grp(i), 0, i % tiles_per_group)),
        row(c_dim),
        pl.BlockSpec((None, 2 * stride, c_dim), lambda i: (i, 0, 0)),
    )
    in_specs = [
        row(d),
        pl.BlockSpec((None, rmod, mod.shape[2]), lambda i: (grp(i), 0, 0)),
        pl.BlockSpec((None, 2 * stride, c_dim), lambda i: (grp(i), 0, 0)),
        full((1, d)), full(wqkv.shape), full(wcv.shape), full(wf.shape), full((N_HEADS, d)),
        full(bqkv.shape), full(bcv.shape), full(bf.shape), full((N_HEADS, 1)),
        full(conv_w.shape), full((1, c_dim)), full(gmat.shape),
    ]
    args = [x, mod, prev, g1.reshape(1, d), wqkv, wcv, wf, wf.T, bqkv, bcv, bf, bf.reshape(-1, 1),
            conv_w, g_conv.reshape(1, -1), gmat]
    aliases = {}
    if kv_buf is not None:
        aliases = {len(args): 3, len(args) + 1: 4}
        in_specs += [pl.BlockSpec(memory_space=pl.ANY)] * 2
        args += list(kv_buf)
        kern = functools.partial(_drop_inputs, kern, len(args) - 2, 2)
    return pl.pallas_call(
        kern,
        grid=(n_tiles,),
        in_specs=in_specs,
        out_specs=out_specs,
        out_shape=out_shape,
        input_output_aliases=aliases,
        scratch_shapes=[pltpu.VMEM((base + tm, c_dim), F32), pltpu.VMEM((N_HEADS, LANES), F32)],
        compiler_params=_cparams(("arbitrary",)),
        name="in_proj",
    )(*args)


def _attn_kernel(qi_ref, kj_ref, q_ref, k_ref, v_ref, nc_ref, g_ref, gm_ref, o_ref,
                 m_sc, l_sc, acc_sc, *, tq, tk, n_pairs):
    hg = pl.program_id(1)
    p = pl.program_id(2)
    qi = qi_ref[p]
    kj = kj_ref[p]

    @pl.when(kj == 0)
    def _():
        m_sc[...] = jnp.full_like(m_sc, NEG)
        l_sc[...] = jnp.zeros_like(l_sc)
        acc_sc[...] = jnp.zeros_like(acc_sc)

    lo_half = lax.broadcasted_iota(jnp.int32, (tq, LANES), 1) < HEAD_DIM
    lo_half_k = lax.broadcasted_iota(jnp.int32, (tk, LANES), 1) < HEAD_DIM
    reps = tk // LANES

    def step(masked):
        for pp in range(n_pairs):
            lanes = slice(pp * LANES, (pp + 1) * LANES)
            q = q_ref[:, lanes]
            k = k_ref[:, lanes]
            v = v_ref[:, lanes]
            one = jnp.ones_like(v)
            alphas, pvs = [], []
            for hh in range(2):
                head = 2 * pp + hh
                mine = lo_half if hh == 0 else jnp.logical_not(lo_half)
                mine_k = lo_half_k if hh == 0 else jnp.logical_not(lo_half_k)
                qm = jnp.where(mine, q, jnp.zeros_like(q))
                s = _dot_nt(qm, k) + nc_ref[pl.ds(2 * n_pairs * hg + head, 1), :]
                if masked:
                    s = jnp.where(lax.broadcasted_iota(jnp.int32, (tq, tk), 1)
                                  > lax.broadcasted_iota(jnp.int32, (tq, tk), 0), NEG, s)
                m_old = m_sc[head]
                m_new = jnp.maximum(m_old, jnp.max(s, axis=-1, keepdims=True))
                alphas.append(jnp.exp2(m_old - m_new))
                pr = jnp.exp2(s - jnp.concatenate([m_new] * reps, axis=1))
                m_sc[head] = m_new
                pvs.append(_dot(pr.astype(BF16), jnp.where(mine_k, v, one)))
            acc_sc[pp] = (jnp.where(lo_half, alphas[0], alphas[1]) * acc_sc[pp]
                          + jnp.where(lo_half, pvs[0], pvs[1]))
            l_sc[pp] = (jnp.where(lo_half, alphas[1], alphas[0]) * l_sc[pp]
                        + jnp.where(lo_half, pvs[1], pvs[0]))

    @pl.when(kj != qi)
    def _():
        step(False)

    @pl.when(kj == qi)
    def _():
        step(True)
        for pp in range(n_pairs):
            lanes = slice(pp * LANES, (pp + 1) * LANES)
            o = acc_sc[pp] / pltpu.roll(l_sc[pp], HEAD_DIM, axis=1)
            msq = _group_mean_sq(o, gm_ref[...])
            o_ref[:, lanes] = (o * lax.rsqrt(msq + EPS) * g_ref[:, lanes]).astype(BF16)


def prompt_attention(qb, kb, vb, negc, g_attn, *, batch, seq, tq, n_pairs):
    n, a_dim = qb.shape
    nq = seq // tq
    wl = n_pairs * LANES
    n_hg = a_dim // wl
    assert a_dim % wl == 0
    blocks = [(i, j) for i in range(nq) for j in range(i + 1)]
    qi = jnp.asarray([p[0] for p in blocks], jnp.int32)
    kj = jnp.asarray([p[1] for p in blocks], jnp.int32)
    gm = _group_matrix(LANES, HEAD_DIM)
    kern = functools.partial(_attn_kernel, tq=tq, tk=tq, n_pairs=n_pairs)
    grid_spec = pltpu.PrefetchScalarGridSpec(
        num_scalar_prefetch=2,
        grid=(batch, n_hg, len(blocks)),
        in_specs=[
            pl.BlockSpec((tq, wl), lambda b, h, p, qi, kj: (b * nq + qi[p], h)),
            pl.BlockSpec((tq, wl), lambda b, h, p, qi, kj: (b * nq + kj[p], h)),
            pl.BlockSpec((tq, wl), lambda b, h, p, qi, kj: (b * nq + kj[p], h)),
            pl.BlockSpec((None, N_HEADS, tq), lambda b, h, p, qi, kj: (b, 0, kj[p])),
            pl.BlockSpec((1, wl), lambda b, h, p, qi, kj: (0, h)),
            pl.BlockSpec((LANES, LANES), lambda b, h, p, qi, kj: (0, 0)),
        ],
        out_specs=pl.BlockSpec((tq, wl), lambda b, h, p, qi, kj: (b * nq + qi[p], h)),
        scratch_shapes=[pltpu.VMEM((2 * n_pairs, tq, LANES), F32), pltpu.VMEM((n_pairs, tq, LANES), F32),
                        pltpu.VMEM((n_pairs, tq, LANES), F32)],
    )
    return pl.pallas_call(
        kern,
        grid_spec=grid_spec,
        out_shape=jax.ShapeDtypeStruct((n, a_dim), BF16),
        compiler_params=_cparams(("arbitrary", "arbitrary", "arbitrary")),
        name="prompt_attn",
    )(qi, kj, qb, kb, vb, negc, g_attn.reshape(1, -1), gm)


def _suffix_kernel(x_ref, o_ref, *, page):
    x = x_ref[...]
    lane = lax.broadcasted_iota(jnp.int32, x.shape, 1)
    c = x
    shift = 1
    while shift < page:
        c = c + jnp.where(lane + shift < page, pltpu.roll(c, page - shift, axis=1), 0.0)
        shift *= 2
    o_ref[:, 0:page] = c - x
    o_ref[:, page:2 * page] = jnp.broadcast_to(c[:, 0:1], x.shape)


def page_suffix_sums(logf_t, tp=2048):
    depth, n_pool, nh, page = logf_t.shape
    rows = depth * n_pool * nh
    while rows % tp:
        tp //= 2
    return pl.pallas_call(
        functools.partial(_suffix_kernel, page=page),
        grid=(rows // tp,),
        in_specs=[pl.BlockSpec((tp, page), lambda i: (i, 0))],
        out_specs=pl.BlockSpec((tp, 2 * page), lambda i: (i, 0)),
        out_shape=jax.ShapeDtypeStruct((rows, 2 * page), F32),
        compiler_params=_cparams(("arbitrary",)),
        name="page_suffix",
    )(logf_t.reshape(rows, page))


def _sample_attn_kernel(pt_ref, q_ref, kn_ref, vn_ref, bn_ref, g_ref, gm_ref, hm_ref, *rest,
                        ppc, n_chunks, n_steps):
    k_refs = rest[0:ppc]
    v_refs = rest[ppc:2 * ppc]
    s_refs = rest[2 * ppc:3 * ppc]
    o_ref = rest[3 * ppc]
    m_sc, l_sc, acc_sc, tail_sc, row_sc = rest[3 * ppc + 1:]
    c = pl.program_id(1)
    q = q_ref[...]
    a_dim = q.shape[-1]
    page = tail_sc.shape[-1]
    wide = a_dim // LANES

    @pl.when(c == 0)
    def _():
        s = _dot_nt(q, kn_ref[...]) + bn_ref[...]
        m = jnp.max(s, axis=-1, keepdims=True)
        pr = jnp.exp(s - m)
        m_sc[...] = jnp.broadcast_to(m, m_sc.shape)
        l_sc[...] = jnp.broadcast_to(jnp.sum(pr, axis=-1, keepdims=True), l_sc.shape)
        acc_sc[...] = _dot(pr.astype(BF16), vn_ref[...])
        tail_sc[...] = jnp.zeros_like(tail_sc)

    tail = tail_sc[...]
    s_parts = [None] * ppc
    for r in reversed(range(ppc)):
        sfx = s_refs[r][...]
        bias = jnp.concatenate([sfx[:, 0:page]] * n_steps, axis=0) + tail
        kt = k_refs[r][...].reshape(a_dim, page).astype(BF16)
        s_parts[r] = _dot(q, kt) + bias
        tail = tail + jnp.concatenate([sfx[:, page:2 * page]] * n_steps, axis=0)
    tail_sc[...] = tail

    m_old = m_sc[...]
    m_cur = s_parts[0]
    for r in range(1, ppc):
        m_cur = jnp.maximum(m_cur, s_parts[r])
    m_new = jnp.maximum(m_old, jnp.max(m_cur, axis=-1, keepdims=True))
    alpha = jnp.exp(m_old - m_new)
    l_new = alpha * l_sc[...]
    pv = None
    for r in range(ppc):
        pr = jnp.exp(s_parts[r] - m_new)
        l_new = l_new + jnp.sum(pr, axis=-1, keepdims=True)
        t = _dot_nt(pr.astype(BF16), v_refs[r][...].reshape(a_dim, page).astype(BF16))
        pv = t if pv is None else pv + t
    m_sc[...] = m_new
    l_sc[...] = l_new
    acc_sc[...] = jnp.concatenate([alpha] * wide, axis=1) * acc_sc[...] + pv

    @pl.when(c == n_chunks - 1)
    def _():
        o = acc_sc[...] / jnp.concatenate([l_sc[...]] * wide, axis=1)
        row_sc[...] = jnp.zeros_like(row_sc)
        for t in range(n_steps):
            blk = o[t * N_HEADS:(t + 1) * N_HEADS, :] * hm_ref[...]
            row_sc[t:t + 1, :] = jnp.sum(blk, axis=0, keepdims=True)
        a = row_sc[...]
        msq = _group_mean_sq(a, gm_ref[...])
        res = a * lax.rsqrt(msq + EPS) * g_ref[...]
        o_ref[...] = res[0:n_steps, :].astype(BF16)


def sample_attention(pt_flat, qbig, knew, vnew, bias_new, g_attn, k_t, v_t, sfx, layer,
                     *, n_pages, ppc):
    bd, rows, a_dim = qbig.shape
    depth, n_pool, nh, dh, page = k_t.shape
    n_steps = rows // nh
    n_chunks = n_pages // ppc
    assert n_pages % ppc == 0 and n_steps <= SUBLANES and page == LANES and nh == SUBLANES
    gm = _group_matrix(a_dim, dh)
    hm = jnp.asarray((np.arange(a_dim)[None, :] // dh == np.arange(nh)[:, None]).astype(np.float32))

    def page_idx(b, c, pt, r):
        return pt[b * n_pages + (n_chunks - 1 - c) * ppc + r]

    def kv_map(r):
        return lambda b, c, pt: (layer, page_idx(b, c, pt, r), 0, 0, 0)

    def sfx_map(r):
        return lambda b, c, pt: (layer * n_pool + page_idx(b, c, pt, r), 0)

    per_b = lambda shape: pl.BlockSpec((None,) + shape, lambda b, c, pt: (b, 0, 0))
    const = lambda shape: pl.BlockSpec(shape, lambda b, c, pt: (0, 0))
    in_specs = [per_b((rows, a_dim)), per_b(knew.shape[1:]), per_b(vnew.shape[1:]), per_b(bias_new.shape[1:]),
                const((1, a_dim)), const(gm.shape), const(hm.shape)]
    in_specs += [pl.BlockSpec((None, None, nh, dh, page), kv_map(r)) for r in range(ppc)]
    in_specs += [pl.BlockSpec((None, None, nh, dh, page), kv_map(r)) for r in range(ppc)]
    in_specs += [pl.BlockSpec((nh, 2 * page), sfx_map(r)) for r in range(ppc)]
    kern = functools.partial(_sample_attn_kernel, ppc=ppc, n_chunks=n_chunks, n_steps=n_steps)
    grid_spec = pltpu.PrefetchScalarGridSpec(
        num_scalar_prefetch=1,
        grid=(bd, n_chunks),
        in_specs=in_specs,
        out_specs=pl.BlockSpec((None, n_steps, a_dim), lambda b, c, pt: (b, 0, 0)),
        scratch_shapes=[pltpu.VMEM((rows, LANES), F32), pltpu.VMEM((rows, LANES), F32),
                        pltpu.VMEM((rows, a_dim), F32), pltpu.VMEM((rows, page), F32),
                        pltpu.VMEM((SUBLANES, a_dim), F32)],
    )
    return pl.pallas_call(
        kern,
        grid_spec=grid_spec,
        out_shape=jax.ShapeDtypeStruct((bd, n_steps, a_dim), BF16),
        compiler_params=_cparams(("arbitrary", "arbitrary")),
        name="sample_attn",
    )(pt_flat, qbig, knew, vnew, bias_new, g_attn.reshape(1, -1), gm, hm,
      *([k_t] * ppc), *([v_t] * ppc), *([sfx] * ppc))


def _outproj_kernel(ma_ref, mc_ref, x_ref, mod_ref, woa_ref, woc_ref, g2_ref, *rest, d, route):
    if route:
        rwcat_ref, rwhi_ref, rb_ref, x1_ref, h2_ref, comb_ref = rest
    else:
        x1_ref, h2_ref = rest
    o = _dot(ma_ref[...], woa_ref[...]) + _dot(mc_ref[...], woc_ref[...])
    x1 = x_ref[...] + mod_ref[:, 2 * d:3 * d] * o
    x1_ref[...] = x1
    ms = jnp.mean(x1 * x1, axis=-1, keepdims=True)
    y = x1 * lax.rsqrt(ms + EPS) * g2_ref[...]
    h2 = y * (1.0 + mod_ref[:, 4 * d:5 * d]) + mod_ref[:, 3 * d:4 * d]
    h2_ref[...] = h2.astype(BF16)
    if route:
        h_hi, h_lo = _split2(h2)
        both = _dot(h_hi, rwcat_ref[...])
        lg = both[:, 0:LANES] + both[:, LANES:2 * LANES] + _dot(h_lo, rwhi_ref[...]) + rb_ref[...]
        lane = lax.broadcasted_iota(jnp.int32, lg.shape, 1)
        m1 = jnp.max(lg, axis=-1, keepdims=True)
        i1 = jnp.min(jnp.where(lg == m1, lane, LANES), axis=-1, keepdims=True)
        lg2 = jnp.where(lane == i1, NEG, lg)
        m2 = jnp.max(lg2, axis=-1, keepdims=True)
        i2 = jnp.min(jnp.where(lg2 == m2, lane, LANES), axis=-1, keepdims=True)
        e2 = jnp.exp(m2 - m1)
        den = 1.0 + e2
        comb_ref[...] = jnp.where(lane == i1, 1.0 / den, 0.0) + jnp.where(lane == i2, e2 / den, 0.0)


def out_projection(m_attn, m_conv, x, mod, w_o, g2, router, *, tm, rows_per_group):
    n, d = x.shape
    a_dim = m_attn.shape[1]
    woa = w_o[0:a_dim].astype(BF16)
    woc = w_o[a_dim:].astype(BF16)
    tiles_per_group = rows_per_group // tm
    rmod = mod.shape[1]
    full = lambda shape: pl.BlockSpec(shape, lambda i: (0,) * len(shape))
    row = lambda w: pl.BlockSpec((tm, w), lambda i: (i, 0))
    in_specs = [row(a_dim), row(m_conv.shape[1]), row(d),
                pl.BlockSpec((None, rmod, mod.shape[2]), lambda i: (i // tiles_per_group, 0, 0)),
                full(woa.shape), full(woc.shape), full((1, d))]
    args = [m_attn, m_conv, x, mod, woa, woc, g2.reshape(1, d)]
    out_shape = [jax.ShapeDtypeStruct((n, d), F32), jax.ShapeDtypeStruct((n, d), BF16)]
    out_specs = [row(d), row(d)]
    if router is not None:
        rw, rb = router
        ne = rw.shape[1]
        rw_hi, rw_lo = _split2(jnp.pad(rw, ((0, 0), (0, LANES - ne))))
        rwcat = jnp.concatenate([rw_hi, rw_lo], axis=1)
        rbp = jnp.concatenate([rb, jnp.full((LANES - ne,), NEG, F32)]).reshape(1, LANES)
        in_specs += [full(rwcat.shape), full(rw_hi.shape), full((1, LANES))]
        args += [rwcat, rw_hi, rbp]
        out_shape.append(jax.ShapeDtypeStruct((n, LANES), F32))
        out_specs.append(row(LANES))
    kern = functools.partial(_outproj_kernel, d=d, route=router is not None)
    return pl.pallas_call(
        kern,
        grid=(n // tm,),
        in_specs=in_specs,
        out_specs=out_specs,
        out_shape=out_shape,
        compiler_params=_cparams(("arbitrary",)),
        name="out_proj",
    )(*args)


def _residual_out(x_ref, mod_ref, acc, fg_ref, o_ref, d, final_norm):
    x2 = x_ref[...] + mod_ref[:, 5 * d:6 * d] * acc[...]
    if final_norm:
        ms = jnp.mean(x2 * x2, axis=-1, keepdims=True)
        x2 = x2 * lax.rsqrt(ms + EPS) * fg_ref[...]
    o_ref[...] = x2


def _ffn_kernel(h_ref, x_ref, mod_ref, wg_ref, wu_ref, wd_ref, fg_ref, o_ref, acc,
                *, d, n_f, final_norm):
    f = pl.program_id(1)

    @pl.when(f == 0)
    def _():
        acc[...] = jnp.zeros_like(acc)

    h = h_ref[...]
    g = _dot(h, wg_ref[...].astype(BF16))
    u = _dot(h, wu_ref[...].astype(BF16))
    a = (jax.nn.silu(g) * u).astype(BF16)
    acc[...] += _dot(a, wd_ref[...].astype(BF16))

    @pl.when(f == n_f - 1)
    def _():
        _residual_out(x_ref, mod_ref, acc, fg_ref, o_ref, d, final_norm)


def ffn_block(h2, x1, mod, wg, wu, wd, final_g, *, tm, tf, rows_per_group, final_norm):
    n, d = x1.shape
    ff = wg.shape[-1]
    n_f = ff // tf
    assert ff % tf == 0 and n % tm == 0
    tiles_per_group = rows_per_group // tm
    rmod = mod.shape[1]
    kern = functools.partial(_ffn_kernel, d=d, n_f=n_f, final_norm=final_norm)
    return pl.pallas_call(
        kern,
        grid=(n // tm, n_f),
        in_specs=[
            pl.BlockSpec((tm, d), lambda i, f: (i, 0)),
            pl.BlockSpec((tm, d), lambda i, f: (i, 0)),
            pl.BlockSpec((None, rmod, mod.shape[2]), lambda i, f: (i // tiles_per_group, 0, 0)),
            pl.BlockSpec((d, tf), lambda i, f: (0, f)),
            pl.BlockSpec((d, tf), lambda i, f: (0, f)),
            pl.BlockSpec((tf, d), lambda i, f: (f, 0)),
            pl.BlockSpec((1, d), lambda i, f: (0, 0)),
        ],
        out_specs=pl.BlockSpec((tm, d), lambda i, f: (i, 0)),
        out_shape=jax.ShapeDtypeStruct((n, d), F32),
        scratch_shapes=[pltpu.VMEM((tm, d), F32)],
        compiler_params=_cparams(("arbitrary", "arbitrary")),
        name="ffn",
    )(h2, x1, mod, wg, wu, wd, final_g.reshape(1, d))


def _moe_kernel(h_ref, x_ref, mod_ref, comb_ref, tri_ref, wg_ref, wu_ref, wd_ref, fg_ref, o_ref,
                acc, rkc, rkr, xc, yc, nch_ref, *, d, n_e, n_f, tm, ch, final_norm):
    e = pl.program_id(1)
    f = pl.program_id(2)

    @pl.when(jnp.logical_and(e == 0, f == 0))
    def _():
        acc[...] = jnp.zeros_like(acc)
        sel = comb_ref[...] > 0.0
        rank = _dot(tri_ref[...], jnp.where(sel, 1.0, 0.0).astype(BF16))
        rk = jnp.where(sel, rank, -1.0)
        rkc[...] = rk
        rkr[...] = rk.T

    @pl.when(f == 0)
    def _():
        rk_row = rkr[pl.ds(e, 1), :]
        cnt = jnp.sum(jnp.where(rk_row >= 0.0, 1.0, 0.0)).astype(jnp.int32)
        nch = (cnt + (ch - 1)) // ch
        nch_ref[0] = nch
        slot = lax.broadcasted_iota(jnp.int32, (ch, tm), 0).astype(F32)

        def pack(c, carry):
            r0 = pl.multiple_of(c * ch, ch)
            onehot = jnp.where(rk_row - (c * ch).astype(F32) == slot, 1.0, 0.0).astype(BF16)
            xc[pl.ds(r0, ch), :] = _dot(onehot, h_ref[...]).astype(BF16)
            yc[pl.ds(r0, ch), :] = jnp.zeros((ch, d), F32)
            return carry

        lax.fori_loop(0, nch, pack, 0)

    nch = nch_ref[0]

    def expert(c, carry):
        r0 = pl.multiple_of(c * ch, ch)
        xb = xc[pl.ds(r0, ch), :]
        g = _dot(xb, wg_ref[...])
        u = _dot(xb, wu_ref[...])
        a = (jax.nn.silu(g) * u).astype(BF16)
        yc[pl.ds(r0, ch), :] += _dot(a, wd_ref[...])
        return carry

    lax.fori_loop(0, nch, expert, 0)

    @pl.when(f == n_f - 1)
    def _():
        lane = lax.broadcasted_iota(jnp.int32, (tm, LANES), 1)
        pick = lane == e
        gate = jnp.sum(jnp.where(pick, comb_ref[...], 0.0), axis=-1, keepdims=True)
        rk_col = jnp.sum(jnp.where(pick, rkc[...], 0.0), axis=-1, keepdims=True)
        slot = lax.broadcasted_iota(jnp.int32, (tm, ch), 1).astype(F32)

        def spread(c, carry):
            r0 = pl.multiple_of(c * ch, ch)
            onehot = jnp.where(rk_col - (c * ch).astype(F32) == slot, 1.0, 0.0).astype(BF16)
            acc[...] += gate * _dot(onehot, yc[pl.ds(r0, ch), :].astype(BF16))
            return carry

        lax.fori_loop(0, nch, spread, 0)

    @pl.when(jnp.logical_and(e == n_e - 1, f == n_f - 1))
    def _():
        _residual_out(x_ref, mod_ref, acc, fg_ref, o_ref, d, final_norm)


def moe_block(h2, x1, mod, comb, wg, wu, wd, final_g, *, tm, tf, ch, rows_per_group, final_norm):
    n, d = x1.shape
    n_e, _, ff = wg.shape
    n_f = ff // tf
    assert ff % tf == 0 and n % tm == 0 and ch % LANES == 0
    cap = -(-tm // ch) * ch
    tiles_per_group = rows_per_group // tm
    rmod = mod.shape[1]
    tri = jnp.asarray(np.tril(np.ones((tm, tm), np.float32), -1), dtype=BF16)
    kern = functools.partial(_moe_kernel, d=d, n_e=n_e, n_f=n_f, tm=tm, ch=ch, final_norm=final_norm)
    return pl.pallas_call(
        kern,
        grid=(n // tm, n_e, n_f),
        in_specs=[
            pl.BlockSpec((tm, d), lambda i, e, f: (i, 0)),
            pl.BlockSpec((tm, d), lambda i, e, f: (i, 0)),
            pl.BlockSpec((None, rmod, mod.shape[2]), lambda i, e, f: (i // tiles_per_group, 0, 0)),
            pl.BlockSpec((tm, LANES), lambda i, e, f: (i, 0)),
            pl.BlockSpec((tm, tm), lambda i, e, f: (0, 0)),
            pl.BlockSpec((None, d, tf), lambda i, e, f: (e, 0, f)),
            pl.BlockSpec((None, d, tf), lambda i, e, f: (e, 0, f)),
            pl.BlockSpec((None, tf, d), lambda i, e, f: (e, f, 0)),
            pl.BlockSpec((1, d), lambda i, e, f: (0, 0)),
        ],
        out_specs=pl.BlockSpec((tm, d), lambda i, e, f: (i, 0)),
        out_shape=jax.ShapeDtypeStruct((n, d), F32),
        scratch_shapes=[pltpu.VMEM((tm, d), F32), pltpu.VMEM((tm, LANES), F32), pltpu.VMEM((LANES, tm), F32),
                        pltpu.VMEM((cap, d), BF16), pltpu.VMEM((cap, d), F32), pltpu.SMEM((1,), jnp.int32)],
        compiler_params=_cparams(("arbitrary", "arbitrary", "arbitrary")),
        name="moe",
    )(h2, x1, mod, comb, tri, wg, wu, wd, final_g.reshape(1, d))


def _pick_tf(ff, target):
    best = LANES
    for t in range(LANES, target + 1, LANES):
        if ff % t == 0:
            best = t
    return best


def kernel(x_prompt, x_sample, cache_k, cache_v, cache_logf, state_conv, page_table, c_prompt, c_sample, w_ada, b_ada, norm1_g, norm2_g, w_in, b_in, conv_w, out_g_attn, out_g_conv, w_o, ffn_w_gate, ffn_w_up, ffn_w_down, router_w, router_b, moe_w_gate, moe_w_up, moe_w_down, final_g):
    batch, seq, d = x_prompt.shape
    bd, ts, _ = x_sample.shape
    depth, n_pool, page, nh, dh = cache_k.shape
    assert (nh, dh) == (N_HEADS, HEAD_DIM)
    a_dim = nh * dh
    c_dim = conv_w.shape[-1]
    n_pages = page_table.shape[1]
    np_rows = batch * seq
    ns_rows = bd * ts
    rows_q = ts * nh

    tm_p = min(512, seq)
    tq = min(512, seq)
    tm_f = min(1024, seq)
    ch_p = 384 if tm_f >= 1024 else LANES
    ppc = 16 if n_pages % 16 == 0 else 8

    n_seq = batch + bd
    pad = -n_seq % SUBLANES
    c_all = jnp.concatenate([c_prompt, c_sample, jnp.zeros((pad, d), F32)], axis=0)
    mods = ada_modulation(c_all, w_ada, b_ada)

    xp = x_prompt.reshape(np_rows, d)
    xs = jnp.transpose(x_sample, (1, 0, 2)).reshape(ns_rows, d)
    k_t = jnp.transpose(cache_k, (0, 1, 3, 4, 2))
    v_t = jnp.transpose(cache_v, (0, 1, 3, 4, 2))
    sfx = page_suffix_sums(jnp.transpose(cache_logf, (0, 1, 3, 2)))
    conv_zero = jnp.zeros((batch, 2, c_dim), F32)
    rq = np.arange(rows_q)
    qmask = jnp.asarray((rq % nh)[:, None] == (np.arange(a_dim) // dh)[None, :])
    causal = jnp.asarray(np.arange(SUBLANES)[None, :] <= (rq // nh)[:, None])

    outs = {k: [] for k in ("lp", "cp", "ls", "cs")}
    kv_p = (jnp.zeros((depth * np_rows * nh, dh), F32),) * 2
    kv_s = (jnp.zeros((depth * ns_rows * nh, dh), F32),) * 2
    for l in range(depth):
        j = l // 2
        last = l == depth - 1
        mod_p = mods[l, 0:batch][:, None, :]
        mod_s = jnp.tile(mods[l, batch:batch + bd], (ts, 1))[None]
        moe = l % 2 == 1
        if moe:
            wg, wu, wd = moe_w_gate[j].astype(BF16), moe_w_up[j].astype(BF16), moe_w_down[j].astype(BF16)
            router = (router_w[j], router_b[j])
        else:
            wg, wu, wd = ffn_w_gate[j], ffn_w_up[j], ffn_w_down[j]
            router = None
        tf = _pick_tf(wg.shape[-1], 1024 if moe else 512)

        def ffn(res, mod, tm, ch, rows_per_group):
            if moe:
                return moe_block(res[1], res[0], mod, res[2], wg, wu, wd, final_g, tm=tm, tf=tf, ch=ch,
                                 rows_per_group=rows_per_group, final_norm=last)
            return ffn_block(res[1], res[0], mod, wg, wu, wd, final_g, tm=tm, tf=tf,
                             rows_per_group=rows_per_group, final_norm=last)

        qb, kb, vb, k1, v1, lf1, negc, mconv, cst = in_projection(
            xp, mod_p, conv_zero, norm1_g[l], w_in[l], b_in[l], conv_w[l], out_g_conv[l],
            tm=tm_p, stride=1, rows_per_group=seq, logit_scale=LOG2E, layer=l, depth=depth, kv_buf=kv_p)
        kv_p = (k1, v1)
        m_attn = prompt_attention(qb, kb, vb, negc, out_g_attn[l], batch=batch, seq=seq, tq=tq, n_pairs=4)
        res = out_projection(m_attn, mconv, xp, mod_p, w_o[l], norm2_g[l], router,
                             tm=tm_p, rows_per_group=seq)
        xp = ffn(res, mod_p, tm_f, ch_p, seq)
        outs["lp"].append(lf1.reshape(batch, seq, nh))
        tiles_per_seq = seq // tm_p
        outs["cp"].append(cst[tiles_per_seq - 1::tiles_per_seq])

        prev_s = jnp.transpose(state_conv[l], (1, 0, 2)).reshape(1, 2 * bd, c_dim)
        qb, kb, vb, k2, v2, lf2, negc, mconv, cst = in_projection(
            xs, mod_s, prev_s, norm1_g[l], w_in[l], b_in[l], conv_w[l], out_g_conv[l],
            tm=ns_rows, stride=bd, rows_per_group=ns_rows, layer=l, depth=depth, kv_buf=kv_s)
        kv_s = (k2, v2)
        to_seq = lambda a: jnp.transpose(a.reshape(ts, bd, -1), (1, 0, 2))
        qbig = jnp.where(qmask[None], jnp.repeat(to_seq(qb), nh, axis=1), jnp.zeros((), BF16))
        rows_pad = ((0, 0), (0, SUBLANES - ts), (0, 0))
        knew = jnp.pad(to_seq(kb), rows_pad)
        vnew = jnp.pad(to_seq(vb), rows_pad)
        nc = jnp.transpose(negc[0].reshape(nh, ts, bd), (2, 0, 1))
        nc = jnp.pad(nc, ((0, 0), (0, 0), (0, SUBLANES - ts)))
        bias_new = jnp.where(causal[None], jnp.tile(nc, (1, ts, 1)), NEG)
        pt_flat = page_table.reshape(-1).astype(jnp.int32)
        a_s = sample_attention(pt_flat, qbig, knew, vnew, bias_new, out_g_attn[l],
                               k_t, v_t, sfx, l, n_pages=n_pages, ppc=ppc)
        m_attn_s = jnp.transpose(a_s, (1, 0, 2)).reshape(ns_rows, a_dim)
        res = out_projection(m_attn_s, mconv, xs, mod_s, w_o[l], norm2_g[l], router,
                             tm=ns_rows, rows_per_group=ns_rows)
        xs = ffn(res, mod_s, ns_rows, LANES, ns_rows)
        outs["ls"].append(to_seq(lf2))
        outs["cs"].append(jnp.transpose(cst.reshape(2, bd, c_dim), (1, 0, 2)))

    y_prompt = xp.reshape(batch, seq, d)
    y_sample = jnp.transpose(xs.reshape(ts, bd, d), (1, 0, 2))
    st = lambda key: jnp.stack(outs[key])
    kv_prompt = lambda a: a.reshape(depth, batch, seq, nh, dh)
    kv_sample = lambda a: jnp.transpose(a.reshape(depth, ts, bd, nh, dh), (0, 2, 1, 3, 4))
    return (y_prompt, y_sample, kv_prompt(kv_p[0]), kv_prompt(kv_p[1]), st("lp"), st("cp"),
            kv_sample(kv_s[0]), kv_sample(kv_s[1]), st("ls"), st("cs"))
```

```python
import functools

import jax
import jax.numpy as jnp
import numpy as np
from jax import lax
from jax.experimental import pallas as pl
from jax.experimental.pallas import tpu as pltpu

N_HEADS = 8
HEAD_DIM = 64
CONV_GROUPS = 8
EPS = 1e-6
NEG = -1e30
LOG2E = 1.4426950408889634
LANES = 128
SUBLANES = 8
VMEM_LIMIT = 56 * 1024 * 1024

F32 = jnp.float32
BF16 = jnp.bfloat16


def _cparams(sem):
    return pltpu.CompilerParams(dimension_semantics=sem, vmem_limit_bytes=VMEM_LIMIT)


def _dot(a, b):
    return jnp.dot(a, b, preferred_element_type=F32)


def _dot_nt(a, b):
    return lax.dot_general(a, b, (((1,), (1,)), ((), ())), preferred_element_type=F32)


def _split2(x):
    hi = x.astype(BF16)
    lo = (x - hi.astype(F32)).astype(BF16)
    return hi, lo


def _split3(x):
    hi = x.astype(BF16)
    r = x - hi.astype(F32)
    mid = r.astype(BF16)
    lo = (r - mid.astype(F32)).astype(BF16)
    return hi, mid, lo


def _log_sigmoid(x):
    return jnp.minimum(x, 0.0) - jnp.log1p(jnp.exp(-jnp.abs(x)))


def _group_mean_sq(z, gmat):
    hi, lo = _split2(z * z)
    return _dot(hi, gmat) + _dot(lo, gmat)


def _drop_inputs(kern, start, count, *refs):
    return kern(*refs[:start], *refs[start + count:])


def _group_matrix(n, group):
    idx = np.arange(n) // group
    return jnp.asarray((idx[:, None] == idx[None, :]).astype(np.float32) / group, dtype=BF16)


def _ada_kernel(c_ref, w_ref, b_ref, o_ref):
    a = jax.nn.silu(c_ref[...]).astype(BF16)
    o_ref[...] = _dot(a, w_ref[...].astype(BF16)) + b_ref[...]


def ada_modulation(c_all, w_ada, b_ada, tn=1536):
    depth, d, n6 = w_ada.shape
    m = c_all.shape[0]
    return pl.pallas_call(
        _ada_kernel,
        grid=(depth, n6 // tn),
        in_specs=[
            pl.BlockSpec((m, d), lambda l, j: (0, 0)),
            pl.BlockSpec((None, d, tn), lambda l, j: (l, 0, j)),
            pl.BlockSpec((None, 1, tn), lambda l, j: (l, 0, j)),
        ],
        out_specs=pl.BlockSpec((None, m, tn), lambda l, j: (l, 0, j)),
        out_shape=jax.ShapeDtypeStruct((depth, m, n6), F32),
        compiler_params=_cparams(("arbitrary", "arbitrary")),
        name="ada_mod",
    )(c_all, w_ada, b_ada.reshape(depth, 1, n6))


def _inproj_kernel(x_ref, mod_ref, prev_ref, g1_ref, wqkv_ref, wcv_ref, wf_ref, wft_ref,
                   bqkv_ref, bcv_ref, bf_ref, bft_ref, cw_ref, gconv_ref, gmat_ref,
                   qb_ref, kb_ref, vb_ref, k_ref, v_ref, logf_ref, negc_ref, mconv_ref, cstate_ref,
                   cbuf, ccarry, *, d, a_dim, c_dim, stride, tiles_per_group, tm, base, logit_scale):
    i = pl.program_id(0)
    first = (i % tiles_per_group) == 0
    x = x_ref[...]
    ms = jnp.mean(x * x, axis=-1, keepdims=True)
    y = x * lax.rsqrt(ms + EPS) * g1_ref[...]
    h = y * (1.0 + mod_ref[:, d:2 * d]) + mod_ref[:, 0:d]
    hb = h.astype(BF16)

    pq = _dot(hb, wqkv_ref[...]) + bqkv_ref[...]
    qb_ref[...] = (pq[:, 0:a_dim] * (HEAD_DIM ** -0.5 * logit_scale)).astype(BF16)
    k = pq[:, a_dim:2 * a_dim]
    v = pq[:, 2 * a_dim:3 * a_dim]
    for hd in range(N_HEADS):
        k_ref[pl.ds(hd, tm, stride=N_HEADS), :] = k[:, hd * HEAD_DIM:(hd + 1) * HEAD_DIM]
        v_ref[pl.ds(hd, tm, stride=N_HEADS), :] = v[:, hd * HEAD_DIM:(hd + 1) * HEAD_DIM]
    kb_ref[...] = k.astype(BF16)
    vb_ref[...] = v.astype(BF16)

    f = _dot(hb, wf_ref[...]) + bf_ref[...]
    logf_ref[...] = _log_sigmoid(f)

    ft = _dot_nt(wft_ref[...], hb) + bft_ref[...]
    c = _log_sigmoid(ft)
    lane = lax.broadcasted_iota(jnp.int32, c.shape, 1)
    shift = stride
    while shift < tm:
        c = c + jnp.where(lane >= shift, pltpu.roll(c, shift, axis=1), 0.0)
        shift *= 2

    @pl.when(first)
    def _():
        ccarry[...] = jnp.zeros_like(ccarry)

    c = c + ccarry[:, 0:1]
    negc_ref[...] = c * (-logit_scale)
    ccarry[...] = jnp.broadcast_to(c[:, tm - 1:tm], ccarry.shape)

    pc = _dot(hb, wcv_ref[...]) + bcv_ref[...]
    gb = pc[:, 0:c_dim]
    cu = pc[:, c_dim:2 * c_dim] * pc[:, 2 * c_dim:3 * c_dim]

    @pl.when(first)
    def _():
        cbuf[base - 2 * stride:base, :] = prev_ref[...]

    cbuf[base:base + tm, :] = cu
    cw = cw_ref[...]
    yc = (cw[0:1, :] * cbuf[base - 2 * stride:base - 2 * stride + tm, :]
          + cw[1:2, :] * cbuf[base - stride:base - stride + tm, :]
          + cw[2:3, :] * cu)
    tail = cbuf[base + tm - 2 * stride:base + tm, :]
    cstate_ref[...] = tail
    cbuf[base - 2 * stride:base, :] = tail

    z = gb * yc
    msq = _group_mean_sq(z, gmat_ref[...])
    mconv_ref[...] = (z * lax.rsqrt(msq + EPS) * gconv_ref[...]).astype(BF16)


def in_projection(x, mod, prev, g1, w_in, b_in, conv_w, g_conv, *, tm, stride, rows_per_group,
                  logit_scale=1.0, layer=0, depth=1, kv_buf=None):
    n, d = x.shape
    a_dim = N_HEADS * HEAD_DIM
    c_dim = conv_w.shape[-1]
    n_tiles = n // tm
    tiles_per_group = rows_per_group // tm
    assert n % tm == 0 and rows_per_group % tm == 0
    assert stride == 1 or tiles_per_group == 1
    n_groups = n // rows_per_group
    base = -(-2 * stride // SUBLANES) * SUBLANES
    f_off = 3 * a_dim
    wqkv = w_in[:, 0:f_off].astype(BF16)
    wf = w_in[:, f_off:f_off + N_HEADS].astype(BF16)
    wcv = w_in[:, f_off + N_HEADS:].astype(BF16)
    bqkv = b_in[0:f_off].reshape(1, -1)
    bf = b_in[f_off:f_off + N_HEADS].reshape(1, -1)
    bcv = b_in[f_off + N_HEADS:].reshape(1, -1)
    gmat = _group_matrix(c_dim, c_dim // CONV_GROUPS)
    rmod = mod.shape[1]

    full = lambda shape: pl.BlockSpec(shape, lambda i: (0,) * len(shape))
    row = lambda w: pl.BlockSpec((tm, w), lambda i: (i, 0))
    grp = lambda i: i // tiles_per_group
    kern = functools.partial(_inproj_kernel, d=d, a_dim=a_dim, c_dim=c_dim, stride=stride,
                             tiles_per_group=tiles_per_group, tm=tm, base=base, logit_scale=logit_scale)
    out_shape = (
        jax.ShapeDtypeStruct((n, a_dim), BF16),
        jax.ShapeDtypeStruct((n, a_dim), BF16),
        jax.ShapeDtypeStruct((n, a_dim), BF16),
        jax.ShapeDtypeStruct((depth * n * N_HEADS, HEAD_DIM), F32),
        jax.ShapeDtypeStruct((depth * n * N_HEADS, HEAD_DIM), F32),
        jax.ShapeDtypeStruct((n, N_HEADS), F32),
        jax.ShapeDtypeStruct((n_groups, N_HEADS, rows_per_group), F32),
        jax.ShapeDtypeStruct((n, c_dim), BF16),
        jax.ShapeDtypeStruct((n_tiles, 2 * stride, c_dim), F32),
    )
    kv_rows = pl.BlockSpec((tm * N_HEADS, HEAD_DIM), lambda i: (layer * n_tiles + i, 0))
    out_specs = (
        row(a_dim), row(a_dim), row(a_dim), kv_rows, kv_rows,
        pl.BlockSpec((tm, N_HEADS), lambda i: (i, 0)),
        pl.BlockSpec((None, N_HEADS, tm), lambda i: (grp(i), 0, i % tiles_per_group)),
        row(c_dim),
        pl.BlockSpec((None, 2 * stride, c_dim), lambda i: (i, 0, 0)),
    )
    in_specs = [
        row(d),
        pl.BlockSpec((None, rmod, mod.shape[2]), lambda i: (grp(i), 0, 0)),
        pl.BlockSpec((None, 2 * stride, c_dim), lambda i: (grp(i), 0, 0)),
        full((1, d)), full(wqkv.shape), full(wcv.shape), full(wf.shape), full((N_HEADS, d)),
        full(bqkv.shape), full(bcv.shape), full(bf.shape), full((N_HEADS, 1)),
        full(conv_w.shape), full((1, c_dim)), full(gmat.shape),
    ]
    args = [x, mod, prev, g1.reshape(1, d), wqkv, wcv, wf, wf.T, bqkv, bcv, bf, bf.reshape(-1, 1),
            conv_w, g_conv.reshape(1, -1), gmat]
    aliases = {}
    if kv_buf is not None:
        aliases = {len(args): 3, len(args) + 1: 4}
        in_specs += [pl.BlockSpec(memory_space=pl.ANY)] * 2
        args += list(kv_buf)
        kern = functools.partial(_drop_inputs, kern, len(args) - 2, 2)
    return pl.pallas_call(
        kern,
        grid=(n_tiles,),
        in_specs=in_specs,
        out_specs=out_specs,
        out_shape=out_shape,
        input_output_aliases=aliases,
        scratch_shapes=[pltpu.VMEM((base + tm, c_dim), F32), pltpu.VMEM((N_HEADS, LANES), F32)],
        compiler_params=_cparams(("arbitrary",)),
        name="in_proj",
    )(*args)


def _attn_kernel(qi_ref, kj_ref, q_ref, k_ref, v_ref, nc_ref, g_ref, gm_ref, o_ref,
                 m_sc, l_sc, acc_sc, *, tq, tk, n_pairs):
    hg = pl.program_id(1)
    p = pl.program_id(2)
    qi = qi_ref[p]
    kj = kj_ref[p]

    @pl.when(kj == 0)
    def _():
        m_sc[...] = jnp.full_like(m_sc, NEG)
        l_sc[...] = jnp.zeros_like(l_sc)
        acc_sc[...] = jnp.zeros_like(acc_sc)

    lo_half = lax.broadcasted_iota(jnp.int32, (tq, LANES), 1) < HEAD_DIM
    lo_half_k = lax.broadcasted_iota(jnp.int32, (tk, LANES), 1) < HEAD_DIM
    reps = tk // LANES

    def step(masked):
        for pp in range(n_pairs):
            lanes = slice(pp * LANES, (pp + 1) * LANES)
            q = q_ref[:, lanes]
            k = k_ref[:, lanes]
            v = v_ref[:, lanes]
            one = jnp.ones_like(v)
            alphas, pvs = [], []
            for hh in range(2):
                head = 2 * pp + hh
                mine = lo_half if hh == 0 else jnp.logical_not(lo_half)
                mine_k = lo_half_k if hh == 0 else jnp.logical_not(lo_half_k)
                qm = jnp.where(mine, q, jnp.zeros_like(q))
                s = _dot_nt(qm, k) + nc_ref[pl.ds(2 * n_pairs * hg + head, 1), :]
                if masked:
                    s = jnp.where(lax.broadcasted_iota(jnp.int32, (tq, tk), 1)
                                  > lax.broadcasted_iota(jnp.int32, (tq, tk), 0), NEG, s)
                m_old = m_sc[head]
                m_new = jnp.maximum(m_old, jnp.max(s, axis=-1, keepdims=True))
                alphas.append(jnp.exp2(m_old - m_new))
                pr = jnp.exp2(s - jnp.concatenate([m_new] * reps, axis=1))
                m_sc[head] = m_new
                pvs.append(_dot(pr.astype(BF16), jnp.where(mine_k, v, one)))
            acc_sc[pp] = (jnp.where(lo_half, alphas[0], alphas[1]) * acc_sc[pp]
                          + jnp.where(lo_half, pvs[0], pvs[1]))
            l_sc[pp] = (jnp.where(lo_half, alphas[1], alphas[0]) * l_sc[pp]
                        + jnp.where(lo_half, pvs[1], pvs[0]))

    @pl.when(kj != qi)
    def _():
        step(False)

    @pl.when(kj == qi)
    def _():
        step(True)
        for pp in range(n_pairs):
            lanes = slice(pp * LANES, (pp + 1) * LANES)
            o = acc_sc[pp] / pltpu.roll(l_sc[pp], HEAD_DIM, axis=1)
            msq = _group_mean_sq(o, gm_ref[...])
            o_ref[:, lanes] = (o * lax.rsqrt(msq + EPS) * g_ref[:, lanes]).astype(BF16)


def prompt_attention(qb, kb, vb, negc, g_attn, *, batch, seq, tq, n_pairs):
    n, a_dim = qb.shape
    nq = seq // tq
    wl = n_pairs * LANES
    n_hg = a_dim // wl
    assert a_dim % wl == 0
    blocks = [(i, j) for i in range(nq) for j in range(i + 1)]
    qi = jnp.asarray([p[0] for p in blocks], jnp.int32)
    kj = jnp.asarray([p[1] for p in blocks], jnp.int32)
    gm = _group_matrix(LANES, HEAD_DIM)
    kern = functools.partial(_attn_kernel, tq=tq, tk=tq, n_pairs=n_pairs)
    grid_spec = pltpu.PrefetchScalarGridSpec(
        num_scalar_prefetch=2,
        grid=(batch, n_hg, len(blocks)),
        in_specs=[
            pl.BlockSpec((tq, wl), lambda b, h, p, qi, kj: (b * nq + qi[p], h)),
            pl.BlockSpec((tq, wl), lambda b, h, p, qi, kj: (b * nq + kj[p], h)),
            pl.BlockSpec((tq, wl), lambda b, h, p, qi, kj: (b * nq + kj[p], h)),
            pl.BlockSpec((None, N_HEADS, tq), lambda b, h, p, qi, kj: (b, 0, kj[p])),
            pl.BlockSpec((1, wl), lambda b, h, p, qi, kj: (0, h)),
            pl.BlockSpec((LANES, LANES), lambda b, h, p, qi, kj: (0, 0)),
        ],
        out_specs=pl.BlockSpec((tq, wl), lambda b, h, p, qi, kj: (b * nq + qi[p], h)),
        scratch_shapes=[pltpu.VMEM((2 * n_pairs, tq, LANES), F32), pltpu.VMEM((n_pairs, tq, LANES), F32),
                        pltpu.VMEM((n_pairs, tq, LANES), F32)],
    )
    return pl.pallas_call(
        kern,
        grid_spec=grid_spec,
        out_shape=jax.ShapeDtypeStruct((n, a_dim), BF16),
        compiler_params=_cparams(("arbitrary", "arbitrary", "arbitrary")),
        name="prompt_attn",
    )(qi, kj, qb, kb, vb, negc, g_attn.reshape(1, -1), gm)


def _suffix_kernel(x_ref, m_ref, o_ref):
    hi, mid, lo = _split3(x_ref[...])
    m = m_ref[...]
    o_ref[...] = _dot(hi, m) + _dot(mid, m) + _dot(lo, m)


def page_suffix_sums(logf_t, tp=2048):
    depth, n_pool, nh, page = logf_t.shape
    rows = depth * n_pool * nh
    while rows % tp:
        tp //= 2
    later = np.arange(page)[:, None] > np.arange(page)[None, :]
    mat = jnp.asarray(np.concatenate([later, np.ones((page, page), bool)], axis=1).astype(np.float32),
                      dtype=BF16)
    return pl.pallas_call(
        _suffix_kernel,
        grid=(rows // tp,),
        in_specs=[pl.BlockSpec((tp, page), lambda i: (i, 0)),
                  pl.BlockSpec((page, 2 * page), lambda i: (0, 0))],
        out_specs=pl.BlockSpec((tp, 2 * page), lambda i: (i, 0)),
        out_shape=jax.ShapeDtypeStruct((rows, 2 * page), F32),
        compiler_params=_cparams(("arbitrary",)),
        name="page_suffix",
    )(logf_t.reshape(rows, page), mat)


def _sample_attn_kernel(pt_ref, q_ref, kn_ref, vn_ref, bn_ref, g_ref, gm_ref, hm_ref, *rest,
                        ppc, n_chunks, n_steps):
    k_refs = rest[0:ppc]
    v_refs = rest[ppc:2 * ppc]
    s_refs = rest[2 * ppc:3 * ppc]
    o_ref = rest[3 * ppc]
    m_sc, l_sc, acc_sc, tail_sc, row_sc = rest[3 * ppc + 1:]
    c = pl.program_id(1)
    q = q_ref[...]
    a_dim = q.shape[-1]
    page = tail_sc.shape[-1]
    wide = a_dim // LANES

    @pl.when(c == 0)
    def _():
        s = _dot_nt(q, kn_ref[...]) + bn_ref[...]
        m = jnp.max(s, axis=-1, keepdims=True)
        pr = jnp.exp(s - m)
        m_sc[...] = jnp.broadcast_to(m, m_sc.shape)
        l_sc[...] = jnp.broadcast_to(jnp.sum(pr, axis=-1, keepdims=True), l_sc.shape)
        acc_sc[...] = _dot(pr.astype(BF16), vn_ref[...])
        tail_sc[...] = jnp.zeros_like(tail_sc)

    tail = tail_sc[...]
    s_parts = [None] * ppc
    for r in reversed(range(ppc)):
        sfx = s_refs[r][...]
        bias = jnp.concatenate([sfx[:, 0:page]] * n_steps, axis=0) + tail
        kt = k_refs[r][...].reshape(a_dim, page).astype(BF16)
        s_parts[r] = _dot(q, kt) + bias
        tail = tail + jnp.concatenate([sfx[:, page:2 * page]] * n_steps, axis=0)
    tail_sc[...] = tail

    m_old = m_sc[...]
    m_cur = s_parts[0]
    for r in range(1, ppc):
        m_cur = jnp.maximum(m_cur, s_parts[r])
    m_new = jnp.maximum(m_old, jnp.max(m_cur, axis=-1, keepdims=True))
    alpha = jnp.exp(m_old - m_new)
    l_new = alpha * l_sc[...]
    pv = None
    for r in range(ppc):
        pr = jnp.exp(s_parts[r] - m_new)
        l_new = l_new + jnp.sum(pr, axis=-1, keepdims=True)
        t = _dot_nt(pr.astype(BF16), v_refs[r][...].reshape(a_dim, page).astype(BF16))
        pv = t if pv is None else pv + t
    m_sc[...] = m_new
    l_sc[...] = l_new
    acc_sc[...] = jnp.concatenate([alpha] * wide, axis=1) * acc_sc[...] + pv

    @pl.when(c == n_chunks - 1)
    def _():
        o = acc_sc[...] / jnp.concatenate([l_sc[...]] * wide, axis=1)
        row_sc[...] = jnp.zeros_like(row_sc)
        for t in range(n_steps):
            blk = o[t * N_HEADS:(t + 1) * N_HEADS, :] * hm_ref[...]
            row_sc[t:t + 1, :] = jnp.sum(blk, axis=0, keepdims=True)
        a = row_sc[...]
        msq = _group_mean_sq(a, gm_ref[...])
        res = a * lax.rsqrt(msq + EPS) * g_ref[...]
        o_ref[...] = res[0:n_steps, :].astype(BF16)


def sample_attention(pt_flat, qbig, knew, vnew, bias_new, g_attn, k_t, v_t, sfx, layer,
                     *, n_pages, ppc):
    bd, rows, a_dim = qbig.shape
    depth, n_pool, nh, dh, page = k_t.shape
    n_steps = rows // nh
    n_chunks = n_pages // ppc
    assert n_pages % ppc == 0 and n_steps <= SUBLANES and page == LANES and nh == SUBLANES
    gm = _group_matrix(a_dim, dh)
    hm = jnp.asarray((np.arange(a_dim)[None, :] // dh == np.arange(nh)[:, None]).astype(np.float32))

    def page_idx(b, c, pt, r):
        return pt[b * n_pages + (n_chunks - 1 - c) * ppc + r]

    def kv_map(r):
        return lambda b, c, pt: (layer, page_idx(b, c, pt, r), 0, 0, 0)

    def sfx_map(r):
        return lambda b, c, pt: (layer * n_pool + page_idx(b, c, pt, r), 0)

    per_b = lambda shape: pl.BlockSpec((None,) + shape, lambda b, c, pt: (b, 0, 0))
    const = lambda shape: pl.BlockSpec(shape, lambda b, c, pt: (0, 0))
    in_specs = [per_b((rows, a_dim)), per_b(knew.shape[1:]), per_b(vnew.shape[1:]), per_b(bias_new.shape[1:]),
                const((1, a_dim)), const(gm.shape), const(hm.shape)]
    in_specs += [pl.BlockSpec((None, None, nh, dh, page), kv_map(r)) for r in range(ppc)]
    in_specs += [pl.BlockSpec((None, None, nh, dh, page), kv_map(r)) for r in range(ppc)]
    in_specs += [pl.BlockSpec((nh, 2 * page), sfx_map(r)) for r in range(ppc)]
    kern = functools.partial(_sample_attn_kernel, ppc=ppc, n_chunks=n_chunks, n_steps=n_steps)
    grid_spec = pltpu.PrefetchScalarGridSpec(
        num_scalar_prefetch=1,
        grid=(bd, n_chunks),
        in_specs=in_specs,
        out_specs=pl.BlockSpec((None, n_steps, a_dim), lambda b, c, pt: (b, 0, 0)),
        scratch_shapes=[pltpu.VMEM((rows, LANES), F32), pltpu.VMEM((rows, LANES), F32),
                        pltpu.VMEM((rows, a_dim), F32), pltpu.VMEM((rows, page), F32),
                        pltpu.VMEM((SUBLANES, a_dim), F32)],
    )
    return pl.pallas_call(
        kern,
        grid_spec=grid_spec,
        out_shape=jax.ShapeDtypeStruct((bd, n_steps, a_dim), BF16),
        compiler_params=_cparams(("arbitrary", "arbitrary")),
        name="sample_attn",
    )(pt_flat, qbig, knew, vnew, bias_new, g_attn.reshape(1, -1), gm, hm,
      *([k_t] * ppc), *([v_t] * ppc), *([sfx] * ppc))


def _outproj_kernel(ma_ref, mc_ref, x_ref, mod_ref, woa_ref, woc_ref, g2_ref, *rest, d, route):
    if route:
        rwcat_ref, rwhi_ref, rb_ref, x1_ref, h2_ref, comb_ref = rest
    else:
        x1_ref, h2_ref = rest
    o = _dot(ma_ref[...], woa_ref[...]) + _dot(mc_ref[...], woc_ref[...])
    x1 = x_ref[...] + mod_ref[:, 2 * d:3 * d] * o
    x1_ref[...] = x1
    ms = jnp.mean(x1 * x1, axis=-1, keepdims=True)
    y = x1 * lax.rsqrt(ms + EPS) * g2_ref[...]
    h2 = y * (1.0 + mod_ref[:, 4 * d:5 * d]) + mod_ref[:, 3 * d:4 * d]
    h2_ref[...] = h2.astype(BF16)
    if route:
        h_hi, h_lo = _split2(h2)
        both = _dot(h_hi, rwcat_ref[...])
        lg = both[:, 0:LANES] + both[:, LANES:2 * LANES] + _dot(h_lo, rwhi_ref[...]) + rb_ref[...]
        lane = lax.broadcasted_iota(jnp.int32, lg.shape, 1)
        m1 = jnp.max(lg, axis=-1, keepdims=True)
        i1 = jnp.min(jnp.where(lg == m1, lane, LANES), axis=-1, keepdims=True)
        lg2 = jnp.where(lane == i1, NEG, lg)
        m2 = jnp.max(lg2, axis=-1, keepdims=True)
        i2 = jnp.min(jnp.where(lg2 == m2, lane, LANES), axis=-1, keepdims=True)
        e2 = jnp.exp(m2 - m1)
        den = 1.0 + e2
        comb_ref[...] = jnp.where(lane == i1, 1.0 / den, 0.0) + jnp.where(lane == i2, e2 / den, 0.0)


def out_projection(m_attn, m_conv, x, mod, w_o, g2, router, *, tm, rows_per_group):
    n, d = x.shape
    a_dim = m_attn.shape[1]
    woa = w_o[0:a_dim].astype(BF16)
    woc = w_o[a_dim:].astype(BF16)
    tiles_per_group = rows_per_group // tm
    rmod = mod.shape[1]
    full = lambda shape: pl.BlockSpec(shape, lambda i: (0,) * len(shape))
    row = lambda w: pl.BlockSpec((tm, w), lambda i: (i, 0))
    in_specs = [row(a_dim), row(m_conv.shape[1]), row(d),
                pl.BlockSpec((None, rmod, mod.shape[2]), lambda i: (i // tiles_per_group, 0, 0)),
                full(woa.shape), full(woc.shape), full((1, d))]
    args = [m_attn, m_conv, x, mod, woa, woc, g2.reshape(1, d)]
    out_shape = [jax.ShapeDtypeStruct((n, d), F32), jax.ShapeDtypeStruct((n, d), BF16)]
    out_specs = [row(d), row(d)]
    if router is not None:
        rw, rb = router
        ne = rw.shape[1]
        rw_hi, rw_lo = _split2(jnp.pad(rw, ((0, 0), (0, LANES - ne))))
        rwcat = jnp.concatenate([rw_hi, rw_lo], axis=1)
        rbp = jnp.concatenate([rb, jnp.full((LANES - ne,), NEG, F32)]).reshape(1, LANES)
        in_specs += [full(rwcat.shape), full(rw_hi.shape), full((1, LANES))]
        args += [rwcat, rw_hi, rbp]
        out_shape.append(jax.ShapeDtypeStruct((n, LANES), F32))
        out_specs.append(row(LANES))
    kern = functools.partial(_outproj_kernel, d=d, route=router is not None)
    return pl.pallas_call(
        kern,
        grid=(n // tm,),
        in_specs=in_specs,
        out_specs=out_specs,
        out_shape=out_shape,
        compiler_params=_cparams(("arbitrary",)),
        name="out_proj",
    )(*args)


def _residual_out(x_ref, mod_ref, acc, fg_ref, o_ref, d, final_norm):
    x2 = x_ref[...] + mod_ref[:, 5 * d:6 * d] * acc[...]
    if final_norm:
        ms = jnp.mean(x2 * x2, axis=-1, keepdims=True)
        x2 = x2 * lax.rsqrt(ms + EPS) * fg_ref[...]
    o_ref[...] = x2


def _ffn_kernel(h_ref, x_ref, mod_ref, wg_ref, wu_ref, wd_ref, fg_ref, o_ref, acc,
                *, d, n_f, final_norm):
    f = pl.program_id(1)

    @pl.when(f == 0)
    def _():
        acc[...] = jnp.zeros_like(acc)

    h = h_ref[...]
    g = _dot(h, wg_ref[...])
    u = _dot(h, wu_ref[...])
    a = (jax.nn.silu(g) * u).astype(BF16)
    acc[...] += _dot(a, wd_ref[...])

    @pl.when(f == n_f - 1)
    def _():
        _residual_out(x_ref, mod_ref, acc, fg_ref, o_ref, d, final_norm)


def ffn_block(h2, x1, mod, wg, wu, wd, final_g, *, tm, tf, rows_per_group, final_norm):
    n, d = x1.shape
    ff = wg.shape[-1]
    n_f = ff // tf
    assert ff % tf == 0 and n % tm == 0
    tiles_per_group = rows_per_group // tm
    rmod = mod.shape[1]
    kern = functools.partial(_ffn_kernel, d=d, n_f=n_f, final_norm=final_norm)
    return pl.pallas_call(
        kern,
        grid=(n // tm, n_f),
        in_specs=[
            pl.BlockSpec((tm, d), lambda i, f: (i, 0)),
            pl.BlockSpec((tm, d), lambda i, f: (i, 0)),
            pl.BlockSpec((None, rmod, mod.shape[2]), lambda i, f: (i // tiles_per_group, 0, 0)),
            pl.BlockSpec((d, tf), lambda i, f: (0, f)),
            pl.BlockSpec((d, tf), lambda i, f: (0, f)),
            pl.BlockSpec((tf, d), lambda i, f: (f, 0)),
            pl.BlockSpec((1, d), lambda i, f: (0, 0)),
        ],
        out_specs=pl.BlockSpec((tm, d), lambda i, f: (i, 0)),
        out_shape=jax.ShapeDtypeStruct((n, d), F32),
        scratch_shapes=[pltpu.VMEM((tm, d), F32)],
        compiler_params=_cparams(("arbitrary", "arbitrary")),
        name="ffn",
    )(h2, x1, mod, wg, wu, wd, final_g.reshape(1, d))


def _for_pieces(n_units, sizes, fn):
    unit = sizes[-1]
    big = sizes[0]
    n_big = n_units // (big // unit)

    def body(c, carry):
        fn(pl.multiple_of(c * big, big), big)
        return carry

    lax.fori_loop(0, n_big, body, 0)
    row0 = n_big * big
    left = n_units - n_big * (big // unit)
    for size in sizes[1:]:
        take = left >= size // unit

        @pl.when(take)
        def _(row0=row0, size=size):
            fn(pl.multiple_of(row0, unit), size)

        row0 = row0 + jnp.where(take, size, 0)
        left = left - jnp.where(take, size // unit, 0)


def _moe_kernel(h_ref, x_ref, mod_ref, comb_ref, tri_ref, wg_ref, wu_ref, wd_ref, fg_ref, o_ref,
                acc, rkc, rkr, xc, yc, nch_ref, *, d, n_e, n_f, tm, sizes, final_norm):
    e = pl.program_id(1)
    f = pl.program_id(2)
    ch = sizes[-1]

    @pl.when(jnp.logical_and(e == 0, f == 0))
    def _():
        acc[...] = jnp.zeros_like(acc)
        sel = comb_ref[...] > 0.0
        rank = _dot(tri_ref[...], jnp.where(sel, 1.0, 0.0).astype(BF16))
        rk = jnp.where(sel, rank, -1.0)
        rkc[...] = rk
        rkr[...] = rk.T

    @pl.when(f == 0)
    def _():
        rk_row = rkr[pl.ds(e, 1), :]
        cnt = jnp.sum(jnp.where(rk_row >= 0.0, 1.0, 0.0)).astype(jnp.int32)
        nch = (cnt + (ch - 1)) // ch
        nch_ref[0] = nch

        def pack(r0, size):
            slot = lax.broadcasted_iota(jnp.int32, (size, tm), 0).astype(F32)
            onehot = jnp.where(rk_row - r0.astype(F32) == slot, 1.0, 0.0).astype(BF16)
            xc[pl.ds(r0, size), :] = _dot(onehot, h_ref[...]).astype(BF16)
            yc[pl.ds(r0, size), :] = jnp.zeros((size, d), F32)

        _for_pieces(nch, sizes, pack)

    nch = nch_ref[0]

    def expert(r0, size):
        xb = xc[pl.ds(r0, size), :]
        g = _dot(xb, wg_ref[...])
        u = _dot(xb, wu_ref[...])
        a = (jax.nn.silu(g) * u).astype(BF16)
        yc[pl.ds(r0, size), :] += _dot(a, wd_ref[...])

    _for_pieces(nch, sizes, expert)

    @pl.when(f == n_f - 1)
    def _():
        lane = lax.broadcasted_iota(jnp.int32, (tm, LANES), 1)
        pick = lane == e
        gate = jnp.sum(jnp.where(pick, comb_ref[...], 0.0), axis=-1, keepdims=True)
        rk_col = jnp.sum(jnp.where(pick, rkc[...], 0.0), axis=-1, keepdims=True)

        def spread(r0, size):
            slot = lax.broadcasted_iota(jnp.int32, (tm, size), 1).astype(F32)
            onehot = jnp.where(rk_col - r0.astype(F32) == slot, 1.0, 0.0).astype(BF16)
            acc[...] += gate * _dot(onehot, yc[pl.ds(r0, size), :].astype(BF16))

        _for_pieces(nch, sizes, spread)

    @pl.when(jnp.logical_and(e == n_e - 1, f == n_f - 1))
    def _():
        _residual_out(x_ref, mod_ref, acc, fg_ref, o_ref, d, final_norm)


def moe_block(h2, x1, mod, comb, wg, wu, wd, final_g, *, tm, tf, rows_per_group, final_norm):
    n, d = x1.shape
    n_e, _, ff = wg.shape
    n_f = ff // tf
    assert ff % tf == 0 and n % tm == 0 and tm % LANES == 0
    sizes = tuple(s for s in (4 * LANES, 2 * LANES, LANES) if s <= tm)
    cap = tm
    tiles_per_group = rows_per_group // tm
    rmod = mod.shape[1]
    tri = jnp.asarray(np.tril(np.ones((tm, tm), np.float32), -1), dtype=BF16)
    kern = functools.partial(_moe_kernel, d=d, n_e=n_e, n_f=n_f, tm=tm, sizes=sizes, final_norm=final_norm)
    return pl.pallas_call(
        kern,
        grid=(n // tm, n_e, n_f),
        in_specs=[
            pl.BlockSpec((tm, d), lambda i, e, f: (i, 0)),
            pl.BlockSpec((tm, d), lambda i, e, f: (i, 0)),
            pl.BlockSpec((None, rmod, mod.shape[2]), lambda i, e, f: (i // tiles_per_group, 0, 0)),
            pl.BlockSpec((tm, LANES), lambda i, e, f: (i, 0)),
            pl.BlockSpec((tm, tm), lambda i, e, f: (0, 0)),
            pl.BlockSpec((None, d, tf), lambda i, e, f: (e, 0, f)),
            pl.BlockSpec((None, d, tf), lambda i, e, f: (e, 0, f)),
            pl.BlockSpec((None, tf, d), lambda i, e, f: (e, f, 0)),
            pl.BlockSpec((1, d), lambda i, e, f: (0, 0)),
        ],
        out_specs=pl.BlockSpec((tm, d), lambda i, e, f: (i, 0)),
        out_shape=jax.ShapeDtypeStruct((n, d), F32),
        scratch_shapes=[pltpu.VMEM((tm, d), F32), pltpu.VMEM((tm, LANES), F32), pltpu.VMEM((LANES, tm), F32),
                        pltpu.VMEM((cap, d), BF16), pltpu.VMEM((cap, d), F32), pltpu.SMEM((1,), jnp.int32)],
        compiler_params=_cparams(("arbitrary", "arbitrary", "arbitrary")),
        name="moe",
    )(h2, x1, mod, comb, tri, wg, wu, wd, final_g.reshape(1, d))


def _pick_tf(ff, target):
    best = LANES
    for t in range(LANES, target + 1, LANES):
        if ff % t == 0:
            best = t
    return best


def kernel(x_prompt, x_sample, cache_k, cache_v, cache_logf, state_conv, page_table, c_prompt, c_sample, w_ada, b_ada, norm1_g, norm2_g, w_in, b_in, conv_w, out_g_attn, out_g_conv, w_o, ffn_w_gate, ffn_w_up, ffn_w_down, router_w, router_b, moe_w_gate, moe_w_up, moe_w_down, final_g):
    batch, seq, d = x_prompt.shape
    bd, ts, _ = x_sample.shape
    depth, n_pool, page, nh, dh = cache_k.shape
    assert (nh, dh) == (N_HEADS, HEAD_DIM)
    a_dim = nh * dh
    c_dim = conv_w.shape[-1]
    n_pages = page_table.shape[1]
    np_rows = batch * seq
    ns_rows = bd * ts
    rows_q = ts * nh

    tm_p = min(512, seq)
    tq = min(512, seq)
    tm_f = min(1024, seq)
    ppc = 16 if n_pages % 16 == 0 else 8

    n_seq = batch + bd
    pad = -n_seq % SUBLANES
    c_all = jnp.concatenate([c_prompt, c_sample, jnp.zeros((pad, d), F32)], axis=0)
    mods = ada_modulation(c_all, w_ada, b_ada)

    xp = x_prompt.reshape(np_rows, d)
    xs = jnp.transpose(x_sample, (1, 0, 2)).reshape(ns_rows, d)
    k_t = jnp.transpose(cache_k, (0, 1, 3, 4, 2))
    v_t = jnp.transpose(cache_v, (0, 1, 3, 4, 2))
    sfx = page_suffix_sums(jnp.transpose(cache_logf, (0, 1, 3, 2)))
    conv_zero = jnp.zeros((batch, 2, c_dim), F32)
    rq = np.arange(rows_q)
    qmask = jnp.asarray((rq % nh)[:, None] == (np.arange(a_dim) // dh)[None, :])
    causal = jnp.asarray(np.arange(SUBLANES)[None, :] <= (rq // nh)[:, None])

    outs = {k: [] for k in ("lp", "cp", "ls", "cs")}
    kv_p = (jnp.zeros((depth * np_rows * nh, dh), F32),) * 2
    kv_s = (jnp.zeros((depth * ns_rows * nh, dh), F32),) * 2
    for l in range(depth):
        j = l // 2
        last = l == depth - 1
        mod_p = mods[l, 0:batch][:, None, :]
        mod_s = jnp.tile(mods[l, batch:batch + bd], (ts, 1))[None]
        moe = l % 2 == 1
        if moe:
            wg, wu, wd = moe_w_gate[j].astype(BF16), moe_w_up[j].astype(BF16), moe_w_down[j].astype(BF16)
            router = (router_w[j], router_b[j])
        else:
            wg, wu, wd = ffn_w_gate[j].astype(BF16), ffn_w_up[j].astype(BF16), ffn_w_down[j].astype(BF16)
            router = None
        tf = _pick_tf(wg.shape[-1], 1024 if moe else 1536)

        def ffn(res, mod, tm, rows_per_group):
            if moe:
                return moe_block(res[1], res[0], mod, res[2], wg, wu, wd, final_g, tm=tm, tf=tf,
                                 rows_per_group=rows_per_group, final_norm=last)
            return ffn_block(res[1], res[0], mod, wg, wu, wd, final_g, tm=tm, tf=tf,
                             rows_per_group=rows_per_group, final_norm=last)

        qb, kb, vb, k1, v1, lf1, negc, mconv, cst = in_projection(
            xp, mod_p, conv_zero, norm1_g[l], w_in[l], b_in[l], conv_w[l], out_g_conv[l],
            tm=tm_p, stride=1, rows_per_group=seq, logit_scale=LOG2E, layer=l, depth=depth, kv_buf=kv_p)
        kv_p = (k1, v1)
        m_attn = prompt_attention(qb, kb, vb, negc, out_g_attn[l], batch=batch, seq=seq, tq=tq, n_pairs=4)
        res = out_projection(m_attn, mconv, xp, mod_p, w_o[l], norm2_g[l], router,
                             tm=tm_p, rows_per_group=seq)
        xp = ffn(res, mod_p, tm_f if moe else tm_p, seq)
        outs["lp"].append(lf1.reshape(batch, seq, nh))
        tiles_per_seq = seq // tm_p
        outs["cp"].append(cst[tiles_per_seq - 1::tiles_per_seq])

        prev_s = jnp.transpose(state_conv[l], (1, 0, 2)).reshape(1, 2 * bd, c_dim)
        qb, kb, vb, k2, v2, lf2, negc, mconv, cst = in_projection(
            xs, mod_s, prev_s, norm1_g[l], w_in[l], b_in[l], conv_w[l], out_g_conv[l],
            tm=ns_rows, stride=bd, rows_per_group=ns_rows, layer=l, depth=depth, kv_buf=kv_s)
        kv_s = (k2, v2)
        to_seq = lambda a: jnp.transpose(a.reshape(ts, bd, -1), (1, 0, 2))
        qbig = jnp.where(qmask[None], jnp.repeat(to_seq(qb), nh, axis=1), jnp.zeros((), BF16))
        rows_pad = ((0, 0), (0, SUBLANES - ts), (0, 0))
        knew = jnp.pad(to_seq(kb), rows_pad)
        vnew = jnp.pad(to_seq(vb), rows_pad)
        nc = jnp.transpose(negc[0].reshape(nh, ts, bd), (2, 0, 1))
        nc = jnp.pad(nc, ((0, 0), (0, 0), (0, SUBLANES - ts)))
        bias_new = jnp.where(causal[None], jnp.tile(nc, (1, ts, 1)), NEG)
        pt_flat = page_table.reshape(-1).astype(jnp.int32)
        a_s = sample_attention(pt_flat, qbig, knew, vnew, bias_new, out_g_attn[l],
                               k_t, v_t, sfx, l, n_pages=n_pages, ppc=ppc)
        m_attn_s = jnp.transpose(a_s, (1, 0, 2)).reshape(ns_rows, a_dim)
        res = out_projection(m_attn_s, mconv, xs, mod_s, w_o[l], norm2_g[l], router,
                             tm=ns_rows, rows_per_group=ns_rows)
        xs = ffn(res, mod_s, ns_rows, ns_rows)
        outs["ls"].append(to_seq(lf2))
        outs["cs"].append(jnp.transpose(cst.reshape(2, bd, c_dim), (1, 0, 2)))

    y_prompt = xp.reshape(batch, seq, d)
    y_sample = jnp.transpose(xs.reshape(ts, bd, d), (1, 0, 2))
    st = lambda key: jnp.stack(outs[key])
    kv_prompt = lambda a: a.reshape(depth, batch, seq, nh, dh)
    kv_sample = lambda a: jnp.transpose(a.reshape(depth, ts, bd, nh, dh), (0, 2, 1, 3, 4))
    return (y_prompt, y_sample, kv_prompt(kv_p[0]), kv_prompt(kv_p[1]), st("lp"), st("cp"),
            kv_sample(kv_s[0]), kv_sample(kv_s[1]), st("ls"), st("cs"))
```

```python
import functools

import jax
import jax.numpy as jnp
import numpy as np
from jax import lax
from jax.experimental import pallas as pl
from jax.experimental.pallas import tpu as pltpu

N_HEADS = 8
HEAD_DIM = 64
CONV_GROUPS = 8
EPS = 1e-6
NEG = -1e30
LOG2E = 1.4426950408889634
LANES = 128
SUBLANES = 8
VMEM_LIMIT = 56 * 1024 * 1024

F32 = jnp.float32
BF16 = jnp.bfloat16


def _cparams(sem):
    return pltpu.CompilerParams(dimension_semantics=sem, vmem_limit_bytes=VMEM_LIMIT)


def _dot(a, b):
    return jnp.dot(a, b, preferred_element_type=F32)


def _dot_nt(a, b):
    return lax.dot_general(a, b, (((1,), (1,)), ((), ())), preferred_element_type=F32)


def _split2(x):
    hi = x.astype(BF16)
    lo = (x - hi.astype(F32)).astype(BF16)
    return hi, lo


def _split3(x):
    hi = x.astype(BF16)
    r = x - hi.astype(F32)
    mid = r.astype(BF16)
    lo = (r - mid.astype(F32)).astype(BF16)
    return hi, mid, lo


def _log_sigmoid(x):
    return jnp.minimum(x, 0.0) - jnp.log1p(jnp.exp(-jnp.abs(x)))


def _group_mean_sq(z, gmat):
    hi, lo = _split2(z * z)
    return _dot(hi, gmat) + _dot(lo, gmat)


def _drop_inputs(kern, start, count, *refs):
    return kern(*refs[:start], *refs[start + count:])


def _group_matrix(n, group):
    idx = np.arange(n) // group
    return jnp.asarray((idx[:, None] == idx[None, :]).astype(np.float32) / group, dtype=BF16)


def _ada_kernel(c_ref, w_ref, b_ref, o_ref):
    a = jax.nn.silu(c_ref[...]).astype(BF16)
    o_ref[...] = _dot(a, w_ref[...].astype(BF16)) + b_ref[...]


def ada_modulation(c_all, w_ada, b_ada, tn=1536):
    depth, d, n6 = w_ada.shape
    m = c_all.shape[0]
    return pl.pallas_call(
        _ada_kernel,
        grid=(depth, n6 // tn),
        in_specs=[
            pl.BlockSpec((m, d), lambda l, j: (0, 0)),
            pl.BlockSpec((None, d, tn), lambda l, j: (l, 0, j)),
            pl.BlockSpec((None, 1, tn), lambda l, j: (l, 0, j)),
        ],
        out_specs=pl.BlockSpec((None, m, tn), lambda l, j: (l, 0, j)),
        out_shape=jax.ShapeDtypeStruct((depth, m, n6), F32),
        compiler_params=_cparams(("arbitrary", "arbitrary")),
        name="ada_mod",
    )(c_all, w_ada, b_ada.reshape(depth, 1, n6))


def _inproj_kernel(x_ref, mod_ref, prev_ref, g1_ref, wqkv_ref, wcv_ref, wf_ref, wft_ref,
                   bqkv_ref, bcv_ref, bf_ref, bft_ref, cw_ref, gconv_ref, gmat_ref,
                   qb_ref, kb_ref, vb_ref, k_ref, v_ref, logf_ref, negc_ref, mconv_ref, cstate_ref,
                   cbuf, ccarry, *, d, a_dim, c_dim, stride, tiles_per_group, tm, base, logit_scale):
    i = pl.program_id(0)
    first = (i % tiles_per_group) == 0
    x = x_ref[...]
    ms = jnp.mean(x * x, axis=-1, keepdims=True)
    y = x * lax.rsqrt(ms + EPS) * g1_ref[...]
    h = y * (1.0 + mod_ref[:, d:2 * d]) + mod_ref[:, 0:d]
    hb = h.astype(BF16)

    pq = _dot(hb, wqkv_ref[...]) + bqkv_ref[...]
    qb_ref[...] = (pq[:, 0:a_dim] * (HEAD_DIM ** -0.5 * logit_scale)).astype(BF16)
    k = pq[:, a_dim:2 * a_dim]
    v = pq[:, 2 * a_dim:3 * a_dim]
    for hd in range(N_HEADS):
        k_ref[pl.ds(hd, tm, stride=N_HEADS), :] = k[:, hd * HEAD_DIM:(hd + 1) * HEAD_DIM]
        v_ref[pl.ds(hd, tm, stride=N_HEADS), :] = v[:, hd * HEAD_DIM:(hd + 1) * HEAD_DIM]
    kb_ref[...] = k.astype(BF16)
    vb_ref[...] = v.astype(BF16)

    f = _dot(hb, wf_ref[...]) + bf_ref[...]
    logf_ref[...] = _log_sigmoid(f)

    ft = _dot_nt(wft_ref[...], hb) + bft_ref[...]
    c = _log_sigmoid(ft)
    lane = lax.broadcasted_iota(jnp.int32, c.shape, 1)
    shift = stride
    while shift < tm:
        c = c + jnp.where(lane >= shift, pltpu.roll(c, shift, axis=1), 0.0)
        shift *= 2

    @pl.when(first)
    def _():
        ccarry[...] = jnp.zeros_like(ccarry)

    c = c + ccarry[:, 0:1]
    negc_ref[...] = c * (-logit_scale)
    ccarry[...] = jnp.broadcast_to(c[:, tm - 1:tm], ccarry.shape)

    pc = _dot(hb, wcv_ref[...]) + bcv_ref[...]
    gb = pc[:, 0:c_dim]
    cu = pc[:, c_dim:2 * c_dim] * pc[:, 2 * c_dim:3 * c_dim]

    @pl.when(first)
    def _():
        cbuf[base - 2 * stride:base, :] = prev_ref[...]

    cbuf[base:base + tm, :] = cu
    cw = cw_ref[...]
    yc = (cw[0:1, :] * cbuf[base - 2 * stride:base - 2 * stride + tm, :]
          + cw[1:2, :] * cbuf[base - stride:base - stride + tm, :]
          + cw[2:3, :] * cu)
    tail = cbuf[base + tm - 2 * stride:base + tm, :]
    cstate_ref[...] = tail
    cbuf[base - 2 * stride:base, :] = tail

    z = gb * yc
    msq = _group_mean_sq(z, gmat_ref[...])
    mconv_ref[...] = (z * lax.rsqrt(msq + EPS) * gconv_ref[...]).astype(BF16)


def in_projection(x, mod, prev, g1, w_in, b_in, conv_w, g_conv, *, tm, stride, rows_per_group,
                  logit_scale=1.0, layer=0, depth=1, kv_buf=None):
    n, d = x.shape
    a_dim = N_HEADS * HEAD_DIM
    c_dim = conv_w.shape[-1]
    n_tiles = n // tm
    tiles_per_group = rows_per_group // tm
    assert n % tm == 0 and rows_per_group % tm == 0
    assert stride == 1 or tiles_per_group == 1
    n_groups = n // rows_per_group
    base = -(-2 * stride // SUBLANES) * SUBLANES
    f_off = 3 * a_dim
    wqkv = w_in[:, 0:f_off].astype(BF16)
    wf = w_in[:, f_off:f_off + N_HEADS].astype(BF16)
    wcv = w_in[:, f_off + N_HEADS:].astype(BF16)
    bqkv = b_in[0:f_off].reshape(1, -1)
    bf = b_in[f_off:f_off + N_HEADS].reshape(1, -1)
    bcv = b_in[f_off + N_HEADS:].reshape(1, -1)
    gmat = _group_matrix(c_dim, c_dim // CONV_GROUPS)
    rmod = mod.shape[1]

    full = lambda shape: pl.BlockSpec(shape, lambda i: (0,) * len(shape))
    row = lambda w: pl.BlockSpec((tm, w), lambda i: (i, 0))
    grp = lambda i: i // tiles_per_group
    kern = functools.partial(_inproj_kernel, d=d, a_dim=a_dim, c_dim=c_dim, stride=stride,
                             tiles_per_group=tiles_per_group, tm=tm, base=base, logit_scale=logit_scale)
    out_shape = (
        jax.ShapeDtypeStruct((n, a_dim), BF16),
        jax.ShapeDtypeStruct((n, a_dim), BF16),
        jax.ShapeDtypeStruct((n, a_dim), BF16),
        jax.ShapeDtypeStruct((depth * n * N_HEADS, HEAD_DIM), F32),
        jax.ShapeDtypeStruct((depth * n * N_HEADS, HEAD_DIM), F32),
        jax.ShapeDtypeStruct((n, N_HEADS), F32),
        jax.ShapeDtypeStruct((n_groups, N_HEADS, rows_per_group), F32),
        jax.ShapeDtypeStruct((n, c_dim), BF16),
        jax.ShapeDtypeStruct((n_tiles, 2 * stride, c_dim), F32),
    )
    kv_rows = pl.BlockSpec((tm * N_HEADS, HEAD_DIM), lambda i: (layer * n_tiles + i, 0))
    out_specs = (
        row(a_dim), row(a_dim), row(a_dim), kv_rows, kv_rows,
        pl.BlockSpec((tm, N_HEADS), lambda i: (i, 0)),
        pl.BlockSpec((None, N_HEADS, tm), lambda i: (grp(i), 0, i % tiles_per_group)),
        row(c_dim),
        pl.BlockSpec((None, 2 * stride, c_dim), lambda i: (i, 0, 0)),
    )
    in_specs = [
        row(d),
        pl.BlockSpec((None, rmod, mod.shape[2]), lambda i: (grp(i), 0, 0)),
        pl.BlockSpec((None, 2 * stride, c_dim), lambda i: (grp(i), 0, 0)),
        full((1, d)), full(wqkv.shape), full(wcv.shape), full(wf.shape), full((N_HEADS, d)),
        full(bqkv.shape), full(bcv.shape), full(bf.shape), full((N_HEADS, 1)),
        full(conv_w.shape), full((1, c_dim)), full(gmat.shape),
    ]
    args = [x, mod, prev, g1.reshape(1, d), wqkv, wcv, wf, wf.T, bqkv, bcv, bf, bf.reshape(-1, 1),
            conv_w, g_conv.reshape(1, -1), gmat]
    aliases = {}
    if kv_buf is not None:
        aliases = {len(args): 3, len(args) + 1: 4}
        in_specs += [pl.BlockSpec(memory_space=pl.ANY)] * 2
        args += list(kv_buf)
        kern = functools.partial(_drop_inputs, kern, len(args) - 2, 2)
    return pl.pallas_call(
        kern,
        grid=(n_tiles,),
        in_specs=in_specs,
        out_specs=out_specs,
        out_shape=out_shape,
        input_output_aliases=aliases,
        scratch_shapes=[pltpu.VMEM((base + tm, c_dim), F32), pltpu.VMEM((N_HEADS, LANES), F32)],
        compiler_params=_cparams(("arbitrary",)),
        name="in_proj",
    )(*args)


def _attn_kernel(qi_ref, kj_ref, q_ref, k_ref, v_ref, nc_ref, g_ref, gm_ref, o_ref,
                 m_sc, l_sc, acc_sc, *, tq, tk, n_pairs):
    hg = pl.program_id(1)
    p = pl.program_id(2)
    qi = qi_ref[p]
    kj = kj_ref[p]

    @pl.when(kj == 0)
    def _():
        m_sc[...] = jnp.full_like(m_sc, NEG)
        l_sc[...] = jnp.zeros_like(l_sc)
        acc_sc[...] = jnp.zeros_like(acc_sc)

    lo_half = lax.broadcasted_iota(jnp.int32, (tq, LANES), 1) < HEAD_DIM
    lo_half_k = lax.broadcasted_iota(jnp.int32, (tk, LANES), 1) < HEAD_DIM
    reps = tk // LANES

    def step(masked):
        for pp in range(n_pairs):
            lanes = slice(pp * LANES, (pp + 1) * LANES)
            q = q_ref[:, lanes]
            k = k_ref[:, lanes]
            v = v_ref[:, lanes]
            one = jnp.ones_like(v)
            alphas, pvs = [], []
            for hh in range(2):
                head = 2 * pp + hh
                mine = lo_half if hh == 0 else jnp.logical_not(lo_half)
                mine_k = lo_half_k if hh == 0 else jnp.logical_not(lo_half_k)
                qm = jnp.where(mine, q, jnp.zeros_like(q))
                s = _dot_nt(qm, k) + nc_ref[pl.ds(2 * n_pairs * hg + head, 1), :]
                if masked:
                    s = jnp.where(lax.broadcasted_iota(jnp.int32, (tq, tk), 1)
                                  > lax.broadcasted_iota(jnp.int32, (tq, tk), 0), NEG, s)
                m_old = m_sc[head]
                m_new = jnp.maximum(m_old, jnp.max(s, axis=-1, keepdims=True))
                alphas.append(jnp.exp2(m_old - m_new))
                pr = jnp.exp2(s - jnp.concatenate([m_new] * reps, axis=1))
                m_sc[head] = m_new
                pvs.append(_dot(pr.astype(BF16), jnp.where(mine_k, v, one)))
            acc_sc[pp] = (jnp.where(lo_half, alphas[0], alphas[1]) * acc_sc[pp]
                          + jnp.where(lo_half, pvs[0], pvs[1]))
            l_sc[pp] = (jnp.where(lo_half, alphas[1], alphas[0]) * l_sc[pp]
                        + jnp.where(lo_half, pvs[1], pvs[0]))

    @pl.when(kj != qi)
    def _():
        step(False)

    @pl.when(kj == qi)
    def _():
        step(True)
        for pp in range(n_pairs):
            lanes = slice(pp * LANES, (pp + 1) * LANES)
            o = acc_sc[pp] / pltpu.roll(l_sc[pp], HEAD_DIM, axis=1)
            msq = _group_mean_sq(o, gm_ref[...])
            o_ref[:, lanes] = (o * lax.rsqrt(msq + EPS) * g_ref[:, lanes]).astype(BF16)


def prompt_attention(qb, kb, vb, negc, g_attn, *, batch, seq, tq, n_pairs):
    n, a_dim = qb.shape
    nq = seq // tq
    wl = n_pairs * LANES
    n_hg = a_dim // wl
    assert a_dim % wl == 0
    blocks = [(i, j) for i in range(nq) for j in range(i + 1)]
    qi = jnp.asarray([p[0] for p in blocks], jnp.int32)
    kj = jnp.asarray([p[1] for p in blocks], jnp.int32)
    gm = _group_matrix(LANES, HEAD_DIM)
    kern = functools.partial(_attn_kernel, tq=tq, tk=tq, n_pairs=n_pairs)
    grid_spec = pltpu.PrefetchScalarGridSpec(
        num_scalar_prefetch=2,
        grid=(batch, n_hg, len(blocks)),
        in_specs=[
            pl.BlockSpec((tq, wl), lambda b, h, p, qi, kj: (b * nq + qi[p], h)),
            pl.BlockSpec((tq, wl), lambda b, h, p, qi, kj: (b * nq + kj[p], h)),
            pl.BlockSpec((tq, wl), lambda b, h, p, qi, kj: (b * nq + kj[p], h)),
            pl.BlockSpec((None, N_HEADS, tq), lambda b, h, p, qi, kj: (b, 0, kj[p])),
            pl.BlockSpec((1, wl), lambda b, h, p, qi, kj: (0, h)),
            pl.BlockSpec((LANES, LANES), lambda b, h, p, qi, kj: (0, 0)),
        ],
        out_specs=pl.BlockSpec((tq, wl), lambda b, h, p, qi, kj: (b * nq + qi[p], h)),
        scratch_shapes=[pltpu.VMEM((2 * n_pairs, tq, LANES), F32), pltpu.VMEM((n_pairs, tq, LANES), F32),
                        pltpu.VMEM((n_pairs, tq, LANES), F32)],
    )
    return pl.pallas_call(
        kern,
        grid_spec=grid_spec,
        out_shape=jax.ShapeDtypeStruct((n, a_dim), BF16),
        compiler_params=_cparams(("arbitrary", "arbitrary", "arbitrary")),
        name="prompt_attn",
    )(qi, kj, qb, kb, vb, negc, g_attn.reshape(1, -1), gm)


def _suffix_kernel(x_ref, m_ref, o_ref):
    hi, mid, lo = _split3(x_ref[...])
    m = m_ref[...]
    o_ref[...] = _dot(hi, m) + _dot(mid, m) + _dot(lo, m)


def page_suffix_sums(logf_t, tp=2048):
    depth, n_pool, nh, page = logf_t.shape
    rows = depth * n_pool * nh
    while rows % tp:
        tp //= 2
    later = np.arange(page)[:, None] > np.arange(page)[None, :]
    mat = jnp.asarray(np.concatenate([later, np.ones((page, page), bool)], axis=1).astype(np.float32),
                      dtype=BF16)
    return pl.pallas_call(
        _suffix_kernel,
        grid=(rows // tp,),
        in_specs=[pl.BlockSpec((tp, page), lambda i: (i, 0)),
                  pl.BlockSpec((page, 2 * page), lambda i: (0, 0))],
        out_specs=pl.BlockSpec((tp, 2 * page), lambda i: (i, 0)),
        out_shape=jax.ShapeDtypeStruct((rows, 2 * page), F32),
        compiler_params=_cparams(("arbitrary",)),
        name="page_suffix",
    )(logf_t.reshape(rows, page), mat)


def _sample_attn_kernel(pt_ref, q_ref, kn_ref, vn_ref, bn_ref, g_ref, gm_ref, hm_ref, *rest,
                        ppc, n_chunks, n_steps):
    k_refs = rest[0:ppc]
    v_refs = rest[ppc:2 * ppc]
    s_refs = rest[2 * ppc:3 * ppc]
    o_ref = rest[3 * ppc]
    m_sc, l_sc, acc_sc, tail_sc, row_sc = rest[3 * ppc + 1:]
    c = pl.program_id(1)
    q = q_ref[...]
    a_dim = q.shape[-1]
    page = tail_sc.shape[-1]
    wide = a_dim // LANES

    @pl.when(c == 0)
    def _():
        s = _dot_nt(q, kn_ref[...]) + bn_ref[...]
        m = jnp.max(s, axis=-1, keepdims=True)
        pr = jnp.exp(s - m)
        m_sc[...] = jnp.broadcast_to(m, m_sc.shape)
        l_sc[...] = jnp.broadcast_to(jnp.sum(pr, axis=-1, keepdims=True), l_sc.shape)
        acc_sc[...] = _dot(pr.astype(BF16), vn_ref[...])
        tail_sc[...] = jnp.zeros_like(tail_sc)

    tail = tail_sc[...]
    s_parts = [None] * ppc
    for r in reversed(range(ppc)):
        sfx = s_refs[r][...]
        bias = jnp.concatenate([sfx[:, 0:page]] * n_steps, axis=0) + tail
        kt = k_refs[r][...].reshape(a_dim, page).astype(BF16)
        s_parts[r] = _dot(q, kt) + bias
        tail = tail + jnp.concatenate([sfx[:, page:2 * page]] * n_steps, axis=0)
    tail_sc[...] = tail

    m_old = m_sc[...]
    m_cur = s_parts[0]
    for r in range(1, ppc):
        m_cur = jnp.maximum(m_cur, s_parts[r])
    m_new = jnp.maximum(m_old, jnp.max(m_cur, axis=-1, keepdims=True))
    alpha = jnp.exp(m_old - m_new)
    l_new = alpha * l_sc[...]
    pv = None
    for r in range(ppc):
        pr = jnp.exp(s_parts[r] - m_new)
        l_new = l_new + jnp.sum(pr, axis=-1, keepdims=True)
        t = _dot_nt(pr.astype(BF16), v_refs[r][...].reshape(a_dim, page).astype(BF16))
        pv = t if pv is None else pv + t
    m_sc[...] = m_new
    l_sc[...] = l_new
    acc_sc[...] = jnp.concatenate([alpha] * wide, axis=1) * acc_sc[...] + pv

    @pl.when(c == n_chunks - 1)
    def _():
        o = acc_sc[...] / jnp.concatenate([l_sc[...]] * wide, axis=1)
        row_sc[...] = jnp.zeros_like(row_sc)
        for t in range(n_steps):
            blk = o[t * N_HEADS:(t + 1) * N_HEADS, :] * hm_ref[...]
            row_sc[t:t + 1, :] = jnp.sum(blk, axis=0, keepdims=True)
        a = row_sc[...]
        msq = _group_mean_sq(a, gm_ref[...])
        res = a * lax.rsqrt(msq + EPS) * g_ref[...]
        o_ref[...] = res[0:n_steps, :].astype(BF16)


def sample_attention(pt_flat, qbig, knew, vnew, bias_new, g_attn, k_t, v_t, sfx, layer,
                     *, n_pages, ppc):
    bd, rows, a_dim = qbig.shape
    depth, n_pool, nh, dh, page = k_t.shape
    n_steps = rows // nh
    n_chunks = n_pages // ppc
    assert n_pages % ppc == 0 and n_steps <= SUBLANES and page == LANES and nh == SUBLANES
    gm = _group_matrix(a_dim, dh)
    hm = jnp.asarray((np.arange(a_dim)[None, :] // dh == np.arange(nh)[:, None]).astype(np.float32))

    def page_idx(b, c, pt, r):
        return pt[b * n_pages + (n_chunks - 1 - c) * ppc + r]

    def kv_map(r):
        return lambda b, c, pt: (layer, page_idx(b, c, pt, r), 0, 0, 0)

    def sfx_map(r):
        return lambda b, c, pt: (layer * n_pool + page_idx(b, c, pt, r), 0)

    per_b = lambda shape: pl.BlockSpec((None,) + shape, lambda b, c, pt: (b, 0, 0))
    const = lambda shape: pl.BlockSpec(shape, lambda b, c, pt: (0, 0))
    in_specs = [per_b((rows, a_dim)), per_b(knew.shape[1:]), per_b(vnew.shape[1:]), per_b(bias_new.shape[1:]),
                const((1, a_dim)), const(gm.shape), const(hm.shape)]
    in_specs += [pl.BlockSpec((None, None, nh, dh, page), kv_map(r)) for r in range(ppc)]
    in_specs += [pl.BlockSpec((None, None, nh, dh, page), kv_map(r)) for r in range(ppc)]
    in_specs += [pl.BlockSpec((nh, 2 * page), sfx_map(r)) for r in range(ppc)]
    kern = functools.partial(_sample_attn_kernel, ppc=ppc, n_chunks=n_chunks, n_steps=n_steps)
    grid_spec = pltpu.PrefetchScalarGridSpec(
        num_scalar_prefetch=1,
        grid=(bd, n_chunks),
        in_specs=in_specs,
        out_specs=pl.BlockSpec((None, n_steps, a_dim), lambda b, c, pt: (b, 0, 0)),
        scratch_shapes=[pltpu.VMEM((rows, LANES), F32), pltpu.VMEM((rows, LANES), F32),
                        pltpu.VMEM((rows, a_dim), F32), pltpu.VMEM((rows, page), F32),
                        pltpu.VMEM((SUBLANES, a_dim), F32)],
    )
    return pl.pallas_call(
        kern,
        grid_spec=grid_spec,
        out_shape=jax.ShapeDtypeStruct((bd, n_steps, a_dim), BF16),
        compiler_params=_cparams(("arbitrary", "arbitrary")),
        name="sample_attn",
    )(pt_flat, qbig, knew, vnew, bias_new, g_attn.reshape(1, -1), gm, hm,
      *([k_t] * ppc), *([v_t] * ppc), *([sfx] * ppc))


def _outproj_kernel(ma_ref, mc_ref, x_ref, mod_ref, woa_ref, woc_ref, g2_ref, *rest, d, route):
    if route:
        rwcat_ref, rwhi_ref, rb_ref, x1_ref, h2_ref, comb_ref = rest
    else:
        x1_ref, h2_ref = rest
    o = _dot(ma_ref[...], woa_ref[...]) + _dot(mc_ref[...], woc_ref[...])
    x1 = x_ref[...] + mod_ref[:, 2 * d:3 * d] * o
    x1_ref[...] = x1
    ms = jnp.mean(x1 * x1, axis=-1, keepdims=True)
    y = x1 * lax.rsqrt(ms + EPS) * g2_ref[...]
    h2 = y * (1.0 + mod_ref[:, 4 * d:5 * d]) + mod_ref[:, 3 * d:4 * d]
    h2_ref[...] = h2.astype(BF16)
    if route:
        h_hi, h_lo = _split2(h2)
        both = _dot(h_hi, rwcat_ref[...])
        lg = both[:, 0:LANES] + both[:, LANES:2 * LANES] + _dot(h_lo, rwhi_ref[...]) + rb_ref[...]
        lane = lax.broadcasted_iota(jnp.int32, lg.shape, 1)
        m1 = jnp.max(lg, axis=-1, keepdims=True)
        i1 = jnp.min(jnp.where(lg == m1, lane, LANES), axis=-1, keepdims=True)
        lg2 = jnp.where(lane == i1, NEG, lg)
        m2 = jnp.max(lg2, axis=-1, keepdims=True)
        i2 = jnp.min(jnp.where(lg2 == m2, lane, LANES), axis=-1, keepdims=True)
        e2 = jnp.exp(m2 - m1)
        den = 1.0 + e2
        comb_ref[...] = jnp.where(lane == i1, 1.0 / den, 0.0) + jnp.where(lane == i2, e2 / den, 0.0)


def out_projection(m_attn, m_conv, x, mod, w_o, g2, router, *, tm, rows_per_group):
    n, d = x.shape
    a_dim = m_attn.shape[1]
    woa = w_o[0:a_dim].astype(BF16)
    woc = w_o[a_dim:].astype(BF16)
    tiles_per_group = rows_per_group // tm
    rmod = mod.shape[1]
    full = lambda shape: pl.BlockSpec(shape, lambda i: (0,) * len(shape))
    row = lambda w: pl.BlockSpec((tm, w), lambda i: (i, 0))
    in_specs = [row(a_dim), row(m_conv.shape[1]), row(d),
                pl.BlockSpec((None, rmod, mod.shape[2]), lambda i: (i // tiles_per_group, 0, 0)),
                full(woa.shape), full(woc.shape), full((1, d))]
    args = [m_attn, m_conv, x, mod, woa, woc, g2.reshape(1, d)]
    out_shape = [jax.ShapeDtypeStruct((n, d), F32), jax.ShapeDtypeStruct((n, d), BF16)]
    out_specs = [row(d), row(d)]
    if router is not None:
        rw, rb = router
        ne = rw.shape[1]
        rw_hi, rw_lo = _split2(jnp.pad(rw, ((0, 0), (0, LANES - ne))))
        rwcat = jnp.concatenate([rw_hi, rw_lo], axis=1)
        rbp = jnp.concatenate([rb, jnp.full((LANES - ne,), NEG, F32)]).reshape(1, LANES)
        in_specs += [full(rwcat.shape), full(rw_hi.shape), full((1, LANES))]
        args += [rwcat, rw_hi, rbp]
        out_shape.append(jax.ShapeDtypeStruct((n, LANES), F32))
        out_specs.append(row(LANES))
    kern = functools.partial(_outproj_kernel, d=d, route=router is not None)
    return pl.pallas_call(
        kern,
        grid=(n // tm,),
        in_specs=in_specs,
        out_specs=out_specs,
        out_shape=out_shape,
        compiler_params=_cparams(("arbitrary",)),
        name="out_proj",
    )(*args)


def _residual_out(x_ref, mod_ref, acc, fg_ref, o_ref, d, final_norm):
    x2 = x_ref[...] + mod_ref[:, 5 * d:6 * d] * acc[...]
    if final_norm:
        ms = jnp.mean(x2 * x2, axis=-1, keepdims=True)
        x2 = x2 * lax.rsqrt(ms + EPS) * fg_ref[...]
    o_ref[...] = x2


def _ffn_kernel(h_ref, x_ref, mod_ref, wg_ref, wu_ref, wd_ref, fg_ref, o_ref, acc,
                *, d, n_f, final_norm):
    f = pl.program_id(1)

    @pl.when(f == 0)
    def _():
        acc[...] = jnp.zeros_like(acc)

    h = h_ref[...]
    g = _dot(h, wg_ref[...])
    u = _dot(h, wu_ref[...])
    a = (jax.nn.silu(g) * u).astype(BF16)
    acc[...] += _dot(a, wd_ref[...])

    @pl.when(f == n_f - 1)
    def _():
        _residual_out(x_ref, mod_ref, acc, fg_ref, o_ref, d, final_norm)


def ffn_block(h2, x1, mod, wg, wu, wd, final_g, *, tm, tf, rows_per_group, final_norm):
    n, d = x1.shape
    ff = wg.shape[-1]
    n_f = ff // tf
    assert ff % tf == 0 and n % tm == 0
    tiles_per_group = rows_per_group // tm
    rmod = mod.shape[1]
    kern = functools.partial(_ffn_kernel, d=d, n_f=n_f, final_norm=final_norm)
    return pl.pallas_call(
        kern,
        grid=(n // tm, n_f),
        in_specs=[
            pl.BlockSpec((tm, d), lambda i, f: (i, 0)),
            pl.BlockSpec((tm, d), lambda i, f: (i, 0)),
            pl.BlockSpec((None, rmod, mod.shape[2]), lambda i, f: (i // tiles_per_group, 0, 0)),
            pl.BlockSpec((d, tf), lambda i, f: (0, f)),
            pl.BlockSpec((d, tf), lambda i, f: (0, f)),
            pl.BlockSpec((tf, d), lambda i, f: (f, 0)),
            pl.BlockSpec((1, d), lambda i, f: (0, 0)),
        ],
        out_specs=pl.BlockSpec((tm, d), lambda i, f: (i, 0)),
        out_shape=jax.ShapeDtypeStruct((n, d), F32),
        scratch_shapes=[pltpu.VMEM((tm, d), F32)],
        compiler_params=_cparams(("arbitrary", "arbitrary")),
        name="ffn",
    )(h2, x1, mod, wg, wu, wd, final_g.reshape(1, d))


def _for_pieces(n_units, sizes, fn):
    unit = sizes[-1]
    big = sizes[0]
    n_big = n_units // (big // unit)

    def body(c, carry):
        fn(pl.multiple_of(c * big, big), big)
        return carry

    lax.fori_loop(0, n_big, body, 0)
    row0 = n_big * big
    left = n_units - n_big * (big // unit)
    for size in sizes[1:]:
        take = left >= size // unit

        @pl.when(take)
        def _(row0=row0, size=size):
            fn(pl.multiple_of(row0, unit), size)

        row0 = row0 + jnp.where(take, size, 0)
        left = left - jnp.where(take, size // unit, 0)


def _moe_kernel(h_ref, x_ref, mod_ref, comb_ref, tri_ref, wg_ref, wu_ref, wd_ref, fg_ref, o_ref,
                acc, rkc, rkr, xc, yc, nch_ref, *, d, n_e, n_f, tm, sizes, final_norm):
    e = pl.program_id(1)
    f = pl.program_id(2)
    ch = sizes[-1]

    @pl.when(jnp.logical_and(e == 0, f == 0))
    def _():
        acc[...] = jnp.zeros_like(acc)
        sel = comb_ref[...] > 0.0
        rank = _dot(tri_ref[...], jnp.where(sel, 1.0, 0.0).astype(BF16))
        rk = jnp.where(sel, rank, -1.0)
        rkc[...] = rk
        rkr[...] = rk.T

    @pl.when(f == 0)
    def _():
        rk_row = rkr[pl.ds(e, 1), :]
        cnt = jnp.sum(jnp.where(rk_row >= 0.0, 1.0, 0.0)).astype(jnp.int32)
        nch = (cnt + (ch - 1)) // ch
        nch_ref[0] = nch

        def pack(r0, size):
            slot = lax.broadcasted_iota(jnp.int32, (size, tm), 0).astype(F32)
            onehot = jnp.where(rk_row - r0.astype(F32) == slot, 1.0, 0.0).astype(BF16)
            xc[pl.ds(r0, size), :] = _dot(onehot, h_ref[...]).astype(BF16)
            yc[pl.ds(r0, size), :] = jnp.zeros((size, d), F32)

        _for_pieces(nch, sizes, pack)

    nch = nch_ref[0]

    def expert(r0, size):
        xb = xc[pl.ds(r0, size), :]
        g = _dot(xb, wg_ref[...])
        u = _dot(xb, wu_ref[...])
        a = (jax.nn.silu(g) * u).astype(BF16)
        yc[pl.ds(r0, size), :] += _dot(a, wd_ref[...])

    _for_pieces(nch, sizes, expert)

    @pl.when(f == n_f - 1)
    def _():
        lane = lax.broadcasted_iota(jnp.int32, (tm, LANES), 1)
        pick = lane == e
        gate = jnp.sum(jnp.where(pick, comb_ref[...], 0.0), axis=-1, keepdims=True)
        rk_col = jnp.sum(jnp.where(pick, rkc[...], 0.0), axis=-1, keepdims=True)

        def spread(r0, size):
            slot = lax.broadcasted_iota(jnp.int32, (tm, size), 1).astype(F32)
            onehot = jnp.where(rk_col - r0.astype(F32) == slot, 1.0, 0.0).astype(BF16)
            acc[...] += gate * _dot(onehot, yc[pl.ds(r0, size), :].astype(BF16))

        _for_pieces(nch, sizes, spread)

    @pl.when(jnp.logical_and(e == n_e - 1, f == n_f - 1))
    def _():
        _residual_out(x_ref, mod_ref, acc, fg_ref, o_ref, d, final_norm)


def moe_block(h2, x1, mod, comb, wg, wu, wd, final_g, *, tm, tf, rows_per_group, final_norm):
    n, d = x1.shape
    n_e, _, ff = wg.shape
    n_f = ff // tf
    assert ff % tf == 0 and n % tm == 0 and tm % LANES == 0
    sizes = tuple(s for s in (4 * LANES, 2 * LANES, LANES) if s <= tm)
    cap = tm
    tiles_per_group = rows_per_group // tm
    rmod = mod.shape[1]
    tri = jnp.asarray(np.tril(np.ones((tm, tm), np.float32), -1), dtype=BF16)
    kern = functools.partial(_moe_kernel, d=d, n_e=n_e, n_f=n_f, tm=tm, sizes=sizes, final_norm=final_norm)
    return pl.pallas_call(
        kern,
        grid=(n // tm, n_e, n_f),
        in_specs=[
            pl.BlockSpec((tm, d), lambda i, e, f: (i, 0)),
            pl.BlockSpec((tm, d), lambda i, e, f: (i, 0)),
            pl.BlockSpec((None, rmod, mod.shape[2]), lambda i, e, f: (i // tiles_per_group, 0, 0)),
            pl.BlockSpec((tm, LANES), lambda i, e, f: (i, 0)),
            pl.BlockSpec((tm, tm), lambda i, e, f: (0, 0)),
            pl.BlockSpec((None, d, tf), lambda i, e, f: (e, 0, f)),
            pl.BlockSpec((None, d, tf), lambda i, e, f: (e, 0, f)),
            pl.BlockSpec((None, tf, d), lambda i, e, f: (e, f, 0)),
            pl.BlockSpec((1, d), lambda i, e, f: (0, 0)),
        ],
        out_specs=pl.BlockSpec((tm, d), lambda i, e, f: (i, 0)),
        out_shape=jax.ShapeDtypeStruct((n, d), F32),
        scratch_shapes=[pltpu.VMEM((tm, d), F32), pltpu.VMEM((tm, LANES), F32), pltpu.VMEM((LANES, tm), F32),
                        pltpu.VMEM((cap, d), BF16), pltpu.VMEM((cap, d), F32), pltpu.SMEM((1,), jnp.int32)],
        compiler_params=_cparams(("arbitrary", "arbitrary", "arbitrary")),
        name="moe",
    )(h2, x1, mod, comb, tri, wg, wu, wd, final_g.reshape(1, d))


def _pick_tf(ff, target):
    best = LANES
    for t in range(LANES, target + 1, LANES):
        if ff % t == 0:
            best = t
    return best


def kernel(x_prompt, x_sample, cache_k, cache_v, cache_logf, state_conv, page_table, c_prompt, c_sample, w_ada, b_ada, norm1_g, norm2_g, w_in, b_in, conv_w, out_g_attn, out_g_conv, w_o, ffn_w_gate, ffn_w_up, ffn_w_down, router_w, router_b, moe_w_gate, moe_w_up, moe_w_down, final_g):
    batch, seq, d = x_prompt.shape
    bd, ts, _ = x_sample.shape
    depth, n_pool, page, nh, dh = cache_k.shape
    assert (nh, dh) == (N_HEADS, HEAD_DIM)
    a_dim = nh * dh
    c_dim = conv_w.shape[-1]
    n_pages = page_table.shape[1]
    np_rows = batch * seq
    ns_rows = bd * ts
    rows_q = ts * nh

    tm_p = min(512, seq)
    tq = min(512, seq)
    tm_f = min(1024, seq)
    ppc = next(p for p in (32, 16, 8) if n_pages % p == 0)

    n_seq = batch + bd
    pad = -n_seq % SUBLANES
    c_all = jnp.concatenate([c_prompt, c_sample, jnp.zeros((pad, d), F32)], axis=0)
    mods = ada_modulation(c_all, w_ada, b_ada)

    xp = x_prompt.reshape(np_rows, d)
    xs = jnp.transpose(x_sample, (1, 0, 2)).reshape(ns_rows, d)
    k_t = jnp.transpose(cache_k, (0, 1, 3, 4, 2))
    v_t = jnp.transpose(cache_v, (0, 1, 3, 4, 2))
    sfx = page_suffix_sums(jnp.transpose(cache_logf, (0, 1, 3, 2)))
    conv_zero = jnp.zeros((batch, 2, c_dim), F32)
    rq = np.arange(rows_q)
    qmask = jnp.asarray((rq % nh)[:, None] == (np.arange(a_dim) // dh)[None, :])
    causal = jnp.asarray(np.arange(SUBLANES)[None, :] <= (rq // nh)[:, None])

    outs = {k: [] for k in ("lp", "cp", "ls", "cs")}
    kv_p = (jnp.zeros((depth * np_rows * nh, dh), F32),) * 2
    kv_s = (jnp.zeros((depth * ns_rows * nh, dh), F32),) * 2
    for l in range(depth):
        j = l // 2
        last = l == depth - 1
        mod_p = mods[l, 0:batch][:, None, :]
        mod_s = jnp.tile(mods[l, batch:batch + bd], (ts, 1))[None]
        moe = l % 2 == 1
        if moe:
            wg, wu, wd = moe_w_gate[j].astype(BF16), moe_w_up[j].astype(BF16), moe_w_down[j].astype(BF16)
            router = (router_w[j], router_b[j])
        else:
            wg, wu, wd = ffn_w_gate[j].astype(BF16), ffn_w_up[j].astype(BF16), ffn_w_down[j].astype(BF16)
            router = None
        tf = _pick_tf(wg.shape[-1], 1024 if moe else 1536)

        def ffn(res, mod, tm, rows_per_group):
            if moe:
                return moe_block(res[1], res[0], mod, res[2], wg, wu, wd, final_g, tm=tm, tf=tf,
                                 rows_per_group=rows_per_group, final_norm=last)
            return ffn_block(res[1], res[0], mod, wg, wu, wd, final_g, tm=tm, tf=tf,
                             rows_per_group=rows_per_group, final_norm=last)

        qb, kb, vb, k1, v1, lf1, negc, mconv, cst = in_projection(
            xp, mod_p, conv_zero, norm1_g[l], w_in[l], b_in[l], conv_w[l], out_g_conv[l],
            tm=tm_p, stride=1, rows_per_group=seq, logit_scale=LOG2E, layer=l, depth=depth, kv_buf=kv_p)
        kv_p = (k1, v1)
        m_attn = prompt_attention(qb, kb, vb, negc, out_g_attn[l], batch=batch, seq=seq, tq=tq, n_pairs=4)
        res = out_projection(m_attn, mconv, xp, mod_p, w_o[l], norm2_g[l], router,
                             tm=tm_p, rows_per_group=seq)
        xp = ffn(res, mod_p, tm_f if moe else tm_p, seq)
        outs["lp"].append(lf1.reshape(batch, seq, nh))
        tiles_per_seq = seq // tm_p
        outs["cp"].append(cst[tiles_per_seq - 1::tiles_per_seq])

        prev_s = jnp.transpose(state_conv[l], (1, 0, 2)).reshape(1, 2 * bd, c_dim)
        qb, kb, vb, k2, v2, lf2, negc, mconv, cst = in_projection(
            xs, mod_s, prev_s, norm1_g[l], w_in[l], b_in[l], conv_w[l], out_g_conv[l],
            tm=ns_rows, stride=bd, rows_per_group=ns_rows, layer=l, depth=depth, kv_buf=kv_s)
        kv_s = (k2, v2)
        to_seq = lambda a: jnp.transpose(a.reshape(ts, bd, -1), (1, 0, 2))
        qbig = jnp.where(qmask[None], jnp.repeat(to_seq(qb), nh, axis=1), jnp.zeros((), BF16))
        rows_pad = ((0, 0), (0, SUBLANES - ts), (0, 0))
        knew = jnp.pad(to_seq(kb), rows_pad)
        vnew = jnp.pad(to_seq(vb), rows_pad)
        nc = jnp.transpose(negc[0].reshape(nh, ts, bd), (2, 0, 1))
        nc = jnp.pad(nc, ((0, 0), (0, 0), (0, SUBLANES - ts)))
        bias_new = jnp.where(causal[None], jnp.tile(nc, (1, ts, 1)), NEG)
        pt_flat = page_table.reshape(-1).astype(jnp.int32)
        a_s = sample_attention(pt_flat, qbig, knew, vnew, bias_new, out_g_attn[l],
                               k_t, v_t, sfx, l, n_pages=n_pages, ppc=ppc)
        m_attn_s = jnp.transpose(a_s, (1, 0, 2)).reshape(ns_rows, a_dim)
        res = out_projection(m_attn_s, mconv, xs, mod_s, w_o[l], norm2_g[l], router,
                             tm=ns_rows, rows_per_group=ns_rows)
        xs = ffn(res, mod_s, ns_rows, ns_rows)
        outs["ls"].append(to_seq(lf2))
        outs["cs"].append(jnp.transpose(cst.reshape(2, bd, c_dim), (1, 0, 2)))

    y_prompt = xp.reshape(batch, seq, d)
    y_sample = jnp.transpose(xs.reshape(ts, bd, d), (1, 0, 2))
    st = lambda key: jnp.stack(outs[key])
    kv_prompt = lambda a: a.reshape(depth, batch, seq, nh, dh)
    kv_sample = lambda a: jnp.transpose(a.reshape(depth, ts, bd, nh, dh), (0, 2, 1, 3, 4))
    return (y_prompt, y_sample, kv_prompt(kv_p[0]), kv_prompt(kv_p[1]), st("lp"), st("cp"),
            kv_sample(kv_s[0]), kv_sample(kv_s[1]), st("ls"), st("cs"))
```

```python
import functools

import jax
import jax.numpy as jnp
import numpy as np
from jax import lax
from jax.experimental import pallas as pl
from jax.experimental.pallas import tpu as pltpu

N_HEADS = 8
HEAD_DIM = 64
CONV_GROUPS = 8
EPS = 1e-6
NEG = -1e30
LOG2E = 1.4426950408889634
LANES = 128
SUBLANES = 8
VMEM_LIMIT = 56 * 1024 * 1024

F32 = jnp.float32
BF16 = jnp.bfloat16


def _cparams(sem):
    return pltpu.CompilerParams(dimension_semantics=sem, vmem_limit_bytes=VMEM_LIMIT)


def _dot(a, b):
    return jnp.dot(a, b, preferred_element_type=F32)


def _dot_nt(a, b):
    return lax.dot_general(a, b, (((1,), (1,)), ((), ())), preferred_element_type=F32)


def _split2(x):
    hi = x.astype(BF16)
    lo = (x - hi.astype(F32)).astype(BF16)
    return hi, lo


def _split3(x):
    hi = x.astype(BF16)
    r = x - hi.astype(F32)
    mid = r.astype(BF16)
    lo = (r - mid.astype(F32)).astype(BF16)
    return hi, mid, lo


def _log_sigmoid(x):
    return jnp.minimum(x, 0.0) - jnp.log1p(jnp.exp(-jnp.abs(x)))


def _group_mean_sq(z, gmat):
    hi, lo = _split2(z * z)
    return _dot(hi, gmat) + _dot(lo, gmat)


def _drop_inputs(kern, start, count, *refs):
    return kern(*refs[:start], *refs[start + count:])


def _group_matrix(n, group):
    idx = np.arange(n) // group
    return jnp.asarray((idx[:, None] == idx[None, :]).astype(np.float32) / group, dtype=BF16)


def _ada_kernel(c_ref, w_ref, b_ref, o_ref):
    a = jax.nn.silu(c_ref[...]).astype(BF16)
    o_ref[...] = _dot(a, w_ref[...].astype(BF16)) + b_ref[...]


def ada_modulation(c_all, w_ada, b_ada, tn=1536):
    depth, d, n6 = w_ada.shape
    m = c_all.shape[0]
    return pl.pallas_call(
        _ada_kernel,
        grid=(depth, n6 // tn),
        in_specs=[
            pl.BlockSpec((m, d), lambda l, j: (0, 0)),
            pl.BlockSpec((None, d, tn), lambda l, j: (l, 0, j)),
            pl.BlockSpec((None, 1, tn), lambda l, j: (l, 0, j)),
        ],
        out_specs=pl.BlockSpec((None, m, tn), lambda l, j: (l, 0, j)),
        out_shape=jax.ShapeDtypeStruct((depth, m, n6), F32),
        compiler_params=_cparams(("arbitrary", "arbitrary")),
        name="ada_mod",
    )(c_all, w_ada, b_ada.reshape(depth, 1, n6))


def _inproj_kernel(x_ref, mod_ref, prev_ref, g1_ref, wqkv_ref, wcv_ref, wf_ref, wft_ref,
                   bqkv_ref, bcv_ref, bf_ref, bft_ref, cw_ref, gconv_ref, gmat_ref,
                   qb_ref, kb_ref, vb_ref, k_ref, v_ref, logf_ref, negc_ref, mconv_ref, cstate_ref,
                   cbuf, ccarry, *, d, a_dim, c_dim, stride, tiles_per_group, tm, base, logit_scale):
    i = pl.program_id(0)
    first = (i % tiles_per_group) == 0
    x = x_ref[...]
    ms = jnp.mean(x * x, axis=-1, keepdims=True)
    y = x * lax.rsqrt(ms + EPS) * g1_ref[...]
    h = y * (1.0 + mod_ref[:, d:2 * d]) + mod_ref[:, 0:d]
    hb = h.astype(BF16)

    pq = _dot(hb, wqkv_ref[...]) + bqkv_ref[...]
    qb_ref[...] = (pq[:, 0:a_dim] * (HEAD_DIM ** -0.5 * logit_scale)).astype(BF16)
    k = pq[:, a_dim:2 * a_dim]
    v = pq[:, 2 * a_dim:3 * a_dim]
    for hd in range(N_HEADS):
        k_ref[pl.ds(hd, tm, stride=N_HEADS), :] = k[:, hd * HEAD_DIM:(hd + 1) * HEAD_DIM]
        v_ref[pl.ds(hd, tm, stride=N_HEADS), :] = v[:, hd * HEAD_DIM:(hd + 1) * HEAD_DIM]
    kb_ref[...] = k.astype(BF16)
    vb_ref[...] = v.astype(BF16)

    f = _dot(hb, wf_ref[...]) + bf_ref[...]
    logf_ref[...] = _log_sigmoid(f)

    ft = _dot_nt(wft_ref[...], hb) + bft_ref[...]
    c = _log_sigmoid(ft)
    lane = lax.broadcasted_iota(jnp.int32, c.shape, 1)
    shift = stride
    while shift < tm:
        c = c + jnp.where(lane >= shift, pltpu.roll(c, shift, axis=1), 0.0)
        shift *= 2

    @pl.when(first)
    def _():
        ccarry[...] = jnp.zeros_like(ccarry)

    c = c + ccarry[:, 0:1]
    negc_ref[...] = c * (-logit_scale)
    ccarry[...] = jnp.broadcast_to(c[:, tm - 1:tm], ccarry.shape)

    pc = _dot(hb, wcv_ref[...]) + bcv_ref[...]
    gb = pc[:, 0:c_dim]
    cu = pc[:, c_dim:2 * c_dim] * pc[:, 2 * c_dim:3 * c_dim]

    @pl.when(first)
    def _():
        cbuf[base - 2 * stride:base, :] = prev_ref[...]

    cbuf[base:base + tm, :] = cu
    cw = cw_ref[...]
    yc = (cw[0:1, :] * cbuf[base - 2 * stride:base - 2 * stride + tm, :]
          + cw[1:2, :] * cbuf[base - stride:base - stride + tm, :]
          + cw[2:3, :] * cu)
    tail = cbuf[base + tm - 2 * stride:base + tm, :]
    cstate_ref[...] = tail
    cbuf[base - 2 * stride:base, :] = tail

    z = gb * yc
    msq = _group_mean_sq(z, gmat_ref[...])
    mconv_ref[...] = (z * lax.rsqrt(msq + EPS) * gconv_ref[...]).astype(BF16)


def in_projection(x, mod, prev, g1, w_in, b_in, conv_w, g_conv, *, tm, stride, rows_per_group,
                  logit_scale=1.0, layer=0, depth=1, kv_buf=None):
    n, d = x.shape
    a_dim = N_HEADS * HEAD_DIM
    c_dim = conv_w.shape[-1]
    n_tiles = n // tm
    tiles_per_group = rows_per_group // tm
    assert n % tm == 0 and rows_per_group % tm == 0
    assert stride == 1 or tiles_per_group == 1
    n_groups = n // rows_per_group
    base = -(-2 * stride // SUBLANES) * SUBLANES
    f_off = 3 * a_dim
    wqkv = w_in[:, 0:f_off].astype(BF16)
    wf = w_in[:, f_off:f_off + N_HEADS].astype(BF16)
    wcv = w_in[:, f_off + N_HEADS:].astype(BF16)
    bqkv = b_in[0:f_off].reshape(1, -1)
    bf = b_in[f_off:f_off + N_HEADS].reshape(1, -1)
    bcv = b_in[f_off + N_HEADS:].reshape(1, -1)
    gmat = _group_matrix(c_dim, c_dim // CONV_GROUPS)
    rmod = mod.shape[1]

    full = lambda shape: pl.BlockSpec(shape, lambda i: (0,) * len(shape))
    row = lambda w: pl.BlockSpec((tm, w), lambda i: (i, 0))
    grp = lambda i: i // tiles_per_group
    kern = functools.partial(_inproj_kernel, d=d, a_dim=a_dim, c_dim=c_dim, stride=stride,
                             tiles_per_group=tiles_per_group, tm=tm, base=base, logit_scale=logit_scale)
    out_shape = (
        jax.ShapeDtypeStruct((n, a_dim), BF16),
        jax.ShapeDtypeStruct((n, a_dim), BF16),
        jax.ShapeDtypeStruct((n, a_dim), BF16),
        jax.ShapeDtypeStruct((depth * n * N_HEADS, HEAD_DIM), F32),
        jax.ShapeDtypeStruct((depth * n * N_HEADS, HEAD_DIM), F32),
        jax.ShapeDtypeStruct((n, N_HEADS), F32),
        jax.ShapeDtypeStruct((n_groups, N_HEADS, rows_per_group), F32),
        jax.ShapeDtypeStruct((n, c_dim), BF16),
        jax.ShapeDtypeStruct((n_tiles, 2 * stride, c_dim), F32),
    )
    kv_rows = pl.BlockSpec((tm * N_HEADS, HEAD_DIM), lambda i: (layer * n_tiles + i, 0))
    out_specs = (
        row(a_dim), row(a_dim), row(a_dim), kv_rows, kv_rows,
        pl.BlockSpec((tm, N_HEADS), lambda i: (i, 0)),
        pl.BlockSpec((None, N_HEADS, tm), lambda i: (grp(i), 0, i % tiles_per_group)),
        row(c_dim),
        pl.BlockSpec((None, 2 * stride, c_dim), lambda i: (i, 0, 0)),
    )
    in_specs = [
        row(d),
        pl.BlockSpec((None, rmod, mod.shape[2]), lambda i: (grp(i), 0, 0)),
        pl.BlockSpec((None, 2 * stride, c_dim), lambda i: (grp(i), 0, 0)),
        full((1, d)), full(wqkv.shape), full(wcv.shape), full(wf.shape), full((N_HEADS, d)),
        full(bqkv.shape), full(bcv.shape), full(bf.shape), full((N_HEADS, 1)),
        full(conv_w.shape), full((1, c_dim)), full(gmat.shape),
    ]
    args = [x, mod, prev, g1.reshape(1, d), wqkv, wcv, wf, wf.T, bqkv, bcv, bf, bf.reshape(-1, 1),
            conv_w, g_conv.reshape(1, -1), gmat]
    aliases = {}
    if kv_buf is not None:
        aliases = {len(args): 3, len(args) + 1: 4}
        in_specs += [pl.BlockSpec(memory_space=pl.ANY)] * 2
        args += list(kv_buf)
        kern = functools.partial(_drop_inputs, kern, len(args) - 2, 2)
    return pl.pallas_call(
        kern,
        grid=(n_tiles,),
        in_specs=in_specs,
        out_specs=out_specs,
        out_shape=out_shape,
        input_output_aliases=aliases,
        scratch_shapes=[pltpu.VMEM((base + tm, c_dim), F32), pltpu.VMEM((N_HEADS, LANES), F32)],
        compiler_params=_cparams(("arbitrary",)),
        name="in_proj",
    )(*args)


def _attn_kernel(qi_ref, kj_ref, q_ref, k_ref, v_ref, nc_ref, g_ref, gm_ref, o_ref,
                 m_sc, l_sc, acc_sc, *, tq, tk, n_pairs):
    hg = pl.program_id(1)
    p = pl.program_id(2)
    qi = qi_ref[p]
    kj = kj_ref[p]

    @pl.when(kj == 0)
    def _():
        m_sc[...] = jnp.full_like(m_sc, NEG)
        l_sc[...] = jnp.zeros_like(l_sc)
        acc_sc[...] = jnp.zeros_like(acc_sc)

    lo_half = lax.broadcasted_iota(jnp.int32, (tq, LANES), 1) < HEAD_DIM
    lo_half_k = lax.broadcasted_iota(jnp.int32, (tk, LANES), 1) < HEAD_DIM
    reps = tk // LANES

    def step(masked):
        for pp in range(n_pairs):
            lanes = slice(pp * LANES, (pp + 1) * LANES)
            q = q_ref[:, lanes]
            k = k_ref[:, lanes]
            v = v_ref[:, lanes]
            one = jnp.ones_like(v)
            alphas, pvs = [], []
            for hh in range(2):
                head = 2 * pp + hh
                mine = lo_half if hh == 0 else jnp.logical_not(lo_half)
                mine_k = lo_half_k if hh == 0 else jnp.logical_not(lo_half_k)
                qm = jnp.where(mine, q, jnp.zeros_like(q))
                s = _dot_nt(qm, k) + nc_ref[pl.ds(2 * n_pairs * hg + head, 1), :]
                if masked:
                    s = jnp.where(lax.broadcasted_iota(jnp.int32, (tq, tk), 1)
                                  > lax.broadcasted_iota(jnp.int32, (tq, tk), 0), NEG, s)
                m_old = m_sc[head]
                m_new = jnp.maximum(m_old, jnp.max(s, axis=-1, keepdims=True))
                alphas.append(jnp.exp2(m_old - m_new))
                pr = jnp.exp2(s - jnp.concatenate([m_new] * reps, axis=1))
                m_sc[head] = m_new
                pvs.append(_dot(pr.astype(BF16), jnp.where(mine_k, v, one)))
            acc_sc[pp] = (jnp.where(lo_half, alphas[0], alphas[1]) * acc_sc[pp]
                          + jnp.where(lo_half, pvs[0], pvs[1]))
            l_sc[pp] = (jnp.where(lo_half, alphas[1], alphas[0]) * l_sc[pp]
                        + jnp.where(lo_half, pvs[1], pvs[0]))

    @pl.when(kj != qi)
    def _():
        step(False)

    @pl.when(kj == qi)
    def _():
        step(True)
        for pp in range(n_pairs):
            lanes = slice(pp * LANES, (pp + 1) * LANES)
            o = acc_sc[pp] / pltpu.roll(l_sc[pp], HEAD_DIM, axis=1)
            msq = _group_mean_sq(o, gm_ref[...])
            o_ref[:, lanes] = (o * lax.rsqrt(msq + EPS) * g_ref[:, lanes]).astype(BF16)


def prompt_attention(qb, kb, vb, negc, g_attn, *, batch, seq, tq, n_pairs):
    n, a_dim = qb.shape
    nq = seq // tq
    wl = n_pairs * LANES
    n_hg = a_dim // wl
    assert a_dim % wl == 0
    blocks = [(i, j) for i in range(nq) for j in range(i + 1)]
    qi = jnp.asarray([p[0] for p in blocks], jnp.int32)
    kj = jnp.asarray([p[1] for p in blocks], jnp.int32)
    gm = _group_matrix(LANES, HEAD_DIM)
    kern = functools.partial(_attn_kernel, tq=tq, tk=tq, n_pairs=n_pairs)
    grid_spec = pltpu.PrefetchScalarGridSpec(
        num_scalar_prefetch=2,
        grid=(batch, n_hg, len(blocks)),
        in_specs=[
            pl.BlockSpec((tq, wl), lambda b, h, p, qi, kj: (b * nq + qi[p], h)),
            pl.BlockSpec((tq, wl), lambda b, h, p, qi, kj: (b * nq + kj[p], h)),
            pl.BlockSpec((tq, wl), lambda b, h, p, qi, kj: (b * nq + kj[p], h)),
            pl.BlockSpec((None, N_HEADS, tq), lambda b, h, p, qi, kj: (b, 0, kj[p])),
            pl.BlockSpec((1, wl), lambda b, h, p, qi, kj: (0, h)),
            pl.BlockSpec((LANES, LANES), lambda b, h, p, qi, kj: (0, 0)),
        ],
        out_specs=pl.BlockSpec((tq, wl), lambda b, h, p, qi, kj: (b * nq + qi[p], h)),
        scratch_shapes=[pltpu.VMEM((2 * n_pairs, tq, LANES), F32), pltpu.VMEM((n_pairs, tq, LANES), F32),
                        pltpu.VMEM((n_pairs, tq, LANES), F32)],
    )
    return pl.pallas_call(
        kern,
        grid_spec=grid_spec,
        out_shape=jax.ShapeDtypeStruct((n, a_dim), BF16),
        compiler_params=_cparams(("arbitrary", "arbitrary", "arbitrary")),
        name="prompt_attn",
    )(qi, kj, qb, kb, vb, negc, g_attn.reshape(1, -1), gm)


def _suffix_kernel(x_ref, m_ref, o_ref):
    hi, mid, lo = _split3(x_ref[...])
    m = m_ref[...]
    o_ref[...] = _dot(hi, m) + _dot(mid, m) + _dot(lo, m)


def page_suffix_sums(logf_t, tp=2048):
    depth, n_pool, nh, page = logf_t.shape
    rows = depth * n_pool * nh
    while rows % tp:
        tp //= 2
    later = np.arange(page)[:, None] > np.arange(page)[None, :]
    mat = jnp.asarray(np.concatenate([later, np.ones((page, page), bool)], axis=1).astype(np.float32),
                      dtype=BF16)
    return pl.pallas_call(
        _suffix_kernel,
        grid=(rows // tp,),
        in_specs=[pl.BlockSpec((tp, page), lambda i: (i, 0)),
                  pl.BlockSpec((page, 2 * page), lambda i: (0, 0))],
        out_specs=pl.BlockSpec((tp, 2 * page), lambda i: (i, 0)),
        out_shape=jax.ShapeDtypeStruct((rows, 2 * page), F32),
        compiler_params=_cparams(("arbitrary",)),
        name="page_suffix",
    )(logf_t.reshape(rows, page), mat)


def _sample_attn_kernel(pt_ref, q_ref, kn_ref, vn_ref, bn_ref, g_ref, gm_ref, hm_ref, *rest,
                        ppc, n_chunks, n_steps):
    k_refs = rest[0:ppc]
    v_refs = rest[ppc:2 * ppc]
    s_refs = rest[2 * ppc:3 * ppc]
    o_ref = rest[3 * ppc]
    m_sc, l_sc, acc_sc, tail_sc, row_sc = rest[3 * ppc + 1:]
    c = pl.program_id(1)
    q = q_ref[...]
    a_dim = q.shape[-1]
    page = tail_sc.shape[-1]
    wide = a_dim // LANES

    @pl.when(c == 0)
    def _():
        s = _dot_nt(q, kn_ref[...]) + bn_ref[...]
        m = jnp.max(s, axis=-1, keepdims=True)
        pr = jnp.exp(s - m)
        m_sc[...] = jnp.broadcast_to(m, m_sc.shape)
        l_sc[...] = jnp.broadcast_to(jnp.sum(pr, axis=-1, keepdims=True), l_sc.shape)
        acc_sc[...] = _dot(pr.astype(BF16), vn_ref[...])
        tail_sc[...] = jnp.zeros_like(tail_sc)

    tail = tail_sc[...]
    s_parts = [None] * ppc
    for r in reversed(range(ppc)):
        sfx = s_refs[r][...]
        bias = jnp.concatenate([sfx[:, 0:page]] * n_steps, axis=0) + tail
        kt = k_refs[r][...].reshape(a_dim, page).astype(BF16)
        s_parts[r] = _dot(q, kt) + bias
        tail = tail + jnp.concatenate([sfx[:, page:2 * page]] * n_steps, axis=0)
    tail_sc[...] = tail

    m_old = m_sc[...]
    m_cur = s_parts[0]
    for r in range(1, ppc):
        m_cur = jnp.maximum(m_cur, s_parts[r])
    m_new = jnp.maximum(m_old, jnp.max(m_cur, axis=-1, keepdims=True))
    alpha = jnp.exp(m_old - m_new)
    l_new = alpha * l_sc[...]
    pv = None
    for r in range(ppc):
        pr = jnp.exp(s_parts[r] - m_new)
        l_new = l_new + jnp.sum(pr, axis=-1, keepdims=True)
        t = _dot_nt(pr.astype(BF16), v_refs[r][...].reshape(a_dim, page).astype(BF16))
        pv = t if pv is None else pv + t
    m_sc[...] = m_new
    l_sc[...] = l_new
    acc_sc[...] = jnp.concatenate([alpha] * wide, axis=1) * acc_sc[...] + pv

    @pl.when(c == n_chunks - 1)
    def _():
        o = acc_sc[...] / jnp.concatenate([l_sc[...]] * wide, axis=1)
        row_sc[...] = jnp.zeros_like(row_sc)
        for t in range(n_steps):
            blk = o[t * N_HEADS:(t + 1) * N_HEADS, :] * hm_ref[...]
            row_sc[t:t + 1, :] = jnp.sum(blk, axis=0, keepdims=True)
        a = row_sc[...]
        msq = _group_mean_sq(a, gm_ref[...])
        res = a * lax.rsqrt(msq + EPS) * g_ref[...]
        o_ref[...] = res[0:n_steps, :].astype(BF16)


def sample_attention(pt_flat, qbig, knew, vnew, bias_new, g_attn, k_t, v_t, sfx, layer,
                     *, n_pages, ppc):
    bd, rows, a_dim = qbig.shape
    depth, n_pool, nh, dh, page = k_t.shape
    n_steps = rows // nh
    n_chunks = n_pages // ppc
    assert n_pages % ppc == 0 and n_steps <= SUBLANES and page == LANES and nh == SUBLANES
    gm = _group_matrix(a_dim, dh)
    hm = jnp.asarray((np.arange(a_dim)[None, :] // dh == np.arange(nh)[:, None]).astype(np.float32))

    def page_idx(b, c, pt, r):
        return pt[b * n_pages + (n_chunks - 1 - c) * ppc + r]

    def kv_map(r):
        return lambda b, c, pt: (layer, page_idx(b, c, pt, r), 0, 0, 0)

    def sfx_map(r):
        return lambda b, c, pt: (layer * n_pool + page_idx(b, c, pt, r), 0)

    per_b = lambda shape: pl.BlockSpec((None,) + shape, lambda b, c, pt: (b, 0, 0))
    const = lambda shape: pl.BlockSpec(shape, lambda b, c, pt: (0, 0))
    in_specs = [per_b((rows, a_dim)), per_b(knew.shape[1:]), per_b(vnew.shape[1:]), per_b(bias_new.shape[1:]),
                const((1, a_dim)), const(gm.shape), const(hm.shape)]
    in_specs += [pl.BlockSpec((None, None, nh, dh, page), kv_map(r)) for r in range(ppc)]
    in_specs += [pl.BlockSpec((None, None, nh, dh, page), kv_map(r)) for r in range(ppc)]
    in_specs += [pl.BlockSpec((nh, 2 * page), sfx_map(r)) for r in range(ppc)]
    kern = functools.partial(_sample_attn_kernel, ppc=ppc, n_chunks=n_chunks, n_steps=n_steps)
    grid_spec = pltpu.PrefetchScalarGridSpec(
        num_scalar_prefetch=1,
        grid=(bd, n_chunks),
        in_specs=in_specs,
        out_specs=pl.BlockSpec((None, n_steps, a_dim), lambda b, c, pt: (b, 0, 0)),
        scratch_shapes=[pltpu.VMEM((rows, LANES), F32), pltpu.VMEM((rows, LANES), F32),
                        pltpu.VMEM((rows, a_dim), F32), pltpu.VMEM((rows, page), F32),
                        pltpu.VMEM((SUBLANES, a_dim), F32)],
    )
    return pl.pallas_call(
        kern,
        grid_spec=grid_spec,
        out_shape=jax.ShapeDtypeStruct((bd, n_steps, a_dim), BF16),
        compiler_params=_cparams(("arbitrary", "arbitrary")),
        name="sample_attn",
    )(pt_flat, qbig, knew, vnew, bias_new, g_attn.reshape(1, -1), gm, hm,
      *([k_t] * ppc), *([v_t] * ppc), *([sfx] * ppc))


def _outproj_kernel(ma_ref, mc_ref, x_ref, mod_ref, woa_ref, woc_ref, g2_ref, *rest, d, route):
    if route:
        rwcat_ref, rwhi_ref, rb_ref, x1_ref, h2_ref, comb_ref = rest
    else:
        x1_ref, h2_ref = rest
    o = _dot(ma_ref[...], woa_ref[...]) + _dot(mc_ref[...], woc_ref[...])
    x1 = x_ref[...] + mod_ref[:, 2 * d:3 * d] * o
    x1_ref[...] = x1
    ms = jnp.mean(x1 * x1, axis=-1, keepdims=True)
    y = x1 * lax.rsqrt(ms + EPS) * g2_ref[...]
    h2 = y * (1.0 + mod_ref[:, 4 * d:5 * d]) + mod_ref[:, 3 * d:4 * d]
    h2_ref[...] = h2.astype(BF16)
    if route:
        h_hi, h_lo = _split2(h2)
        both = _dot(h_hi, rwcat_ref[...])
        lg = both[:, 0:LANES] + both[:, LANES:2 * LANES] + _dot(h_lo, rwhi_ref[...]) + rb_ref[...]
        lane = lax.broadcasted_iota(jnp.int32, lg.shape, 1)
        m1 = jnp.max(lg, axis=-1, keepdims=True)
        i1 = jnp.min(jnp.where(lg == m1, lane, LANES), axis=-1, keepdims=True)
        lg2 = jnp.where(lane == i1, NEG, lg)
        m2 = jnp.max(lg2, axis=-1, keepdims=True)
        i2 = jnp.min(jnp.where(lg2 == m2, lane, LANES), axis=-1, keepdims=True)
        e2 = jnp.exp(m2 - m1)
        den = 1.0 + e2
        comb_ref[...] = jnp.where(lane == i1, 1.0 / den, 0.0) + jnp.where(lane == i2, e2 / den, 0.0)


def out_projection(m_attn, m_conv, x, mod, w_o, g2, router, *, tm, rows_per_group):
    n, d = x.shape
    a_dim = m_attn.shape[1]
    woa = w_o[0:a_dim].astype(BF16)
    woc = w_o[a_dim:].astype(BF16)
    tiles_per_group = rows_per_group // tm
    rmod = mod.shape[1]
    full = lambda shape: pl.BlockSpec(shape, lambda i: (0,) * len(shape))
    row = lambda w: pl.BlockSpec((tm, w), lambda i: (i, 0))
    in_specs = [row(a_dim), row(m_conv.shape[1]), row(d),
                pl.BlockSpec((None, rmod, mod.shape[2]), lambda i: (i // tiles_per_group, 0, 0)),
                full(woa.shape), full(woc.shape), full((1, d))]
    args = [m_attn, m_conv, x, mod, woa, woc, g2.reshape(1, d)]
    out_shape = [jax.ShapeDtypeStruct((n, d), F32), jax.ShapeDtypeStruct((n, d), BF16)]
    out_specs = [row(d), row(d)]
    if router is not None:
        rw, rb = router
        ne = rw.shape[1]
        rw_hi, rw_lo = _split2(jnp.pad(rw, ((0, 0), (0, LANES - ne))))
        rwcat = jnp.concatenate([rw_hi, rw_lo], axis=1)
        rbp = jnp.concatenate([rb, jnp.full((LANES - ne,), NEG, F32)]).reshape(1, LANES)
        in_specs += [full(rwcat.shape), full(rw_hi.shape), full((1, LANES))]
        args += [rwcat, rw_hi, rbp]
        out_shape.append(jax.ShapeDtypeStruct((n, LANES), F32))
        out_specs.append(row(LANES))
    kern = functools.partial(_outproj_kernel, d=d, route=router is not None)
    return pl.pallas_call(
        kern,
        grid=(n // tm,),
        in_specs=in_specs,
        out_specs=out_specs,
        out_shape=out_shape,
        compiler_params=_cparams(("arbitrary",)),
        name="out_proj",
    )(*args)


def _residual_out(x_ref, mod_ref, acc, fg_ref, o_ref, d, final_norm):
    x2 = x_ref[...] + mod_ref[:, 5 * d:6 * d] * acc[...]
    if final_norm:
        ms = jnp.mean(x2 * x2, axis=-1, keepdims=True)
        x2 = x2 * lax.rsqrt(ms + EPS) * fg_ref[...]
    o_ref[...] = x2


def _ffn_kernel(h_ref, x_ref, mod_ref, wg_ref, wu_ref, wd_ref, fg_ref, o_ref, acc,
                *, d, n_f, final_norm):
    f = pl.program_id(1)

    @pl.when(f == 0)
    def _():
        acc[...] = jnp.zeros_like(acc)

    h = h_ref[...]
    g = _dot(h, wg_ref[...])
    u = _dot(h, wu_ref[...])
    a = (jax.nn.silu(g) * u).astype(BF16)
    acc[...] += _dot(a, wd_ref[...])

    @pl.when(f == n_f - 1)
    def _():
        _residual_out(x_ref, mod_ref, acc, fg_ref, o_ref, d, final_norm)


def ffn_block(h2, x1, mod, wg, wu, wd, final_g, *, tm, tf, rows_per_group, final_norm):
    n, d = x1.shape
    ff = wg.shape[-1]
    n_f = ff // tf
    assert ff % tf == 0 and n % tm == 0
    tiles_per_group = rows_per_group // tm
    rmod = mod.shape[1]
    kern = functools.partial(_ffn_kernel, d=d, n_f=n_f, final_norm=final_norm)
    return pl.pallas_call(
        kern,
        grid=(n // tm, n_f),
        in_specs=[
            pl.BlockSpec((tm, d), lambda i, f: (i, 0)),
            pl.BlockSpec((tm, d), lambda i, f: (i, 0)),
            pl.BlockSpec((None, rmod, mod.shape[2]), lambda i, f: (i // tiles_per_group, 0, 0)),
            pl.BlockSpec((d, tf), lambda i, f: (0, f)),
            pl.BlockSpec((d, tf), lambda i, f: (0, f)),
            pl.BlockSpec((tf, d), lambda i, f: (f, 0)),
            pl.BlockSpec((1, d), lambda i, f: (0, 0)),
        ],
        out_specs=pl.BlockSpec((tm, d), lambda i, f: (i, 0)),
        out_shape=jax.ShapeDtypeStruct((n, d), F32),
        scratch_shapes=[pltpu.VMEM((tm, d), F32)],
        compiler_params=_cparams(("arbitrary", "arbitrary")),
        name="ffn",
    )(h2, x1, mod, wg, wu, wd, final_g.reshape(1, d))


def _for_pieces(n_units, sizes, fn):
    unit = sizes[-1]
    big = sizes[0]
    n_big = n_units // (big // unit)

    def body(c, carry):
        fn(pl.multiple_of(c * big, big), big)
        return carry

    lax.fori_loop(0, n_big, body, 0)
    row0 = n_big * big
    left = n_units - n_big * (big // unit)
    for size in sizes[1:]:
        take = left >= size // unit

        @pl.when(take)
        def _(row0=row0, size=size):
            fn(pl.multiple_of(row0, unit), size)

        row0 = row0 + jnp.where(take, size, 0)
        left = left - jnp.where(take, size // unit, 0)


def _moe_kernel(h_ref, x_ref, mod_ref, comb_ref, tri_ref, wg_ref, wu_ref, wd_ref, fg_ref, o_ref,
                acc, rkc, rkr, xc, yc, nch_ref, *, d, n_e, n_f, tm, sizes, final_norm):
    e = pl.program_id(1)
    f = pl.program_id(2)
    ch = sizes[-1]

    @pl.when(jnp.logical_and(e == 0, f == 0))
    def _():
        acc[...] = jnp.zeros_like(acc)
        sel = comb_ref[...] > 0.0
        rank = _dot(tri_ref[...], jnp.where(sel, 1.0, 0.0).astype(BF16))
        rk = jnp.where(sel, rank, -1.0)
        rkc[...] = rk
        rkr[...] = rk.T

    @pl.when(f == 0)
    def _():
        rk_row = rkr[pl.ds(e, 1), :]
        cnt = jnp.sum(jnp.where(rk_row >= 0.0, 1.0, 0.0)).astype(jnp.int32)
        nch = (cnt + (ch - 1)) // ch
        nch_ref[0] = nch

        def pack(r0, size):
            slot = lax.broadcasted_iota(jnp.int32, (size, tm), 0).astype(F32)
            onehot = jnp.where(rk_row - r0.astype(F32) == slot, 1.0, 0.0).astype(BF16)
            xc[pl.ds(r0, size), :] = _dot(onehot, h_ref[...]).astype(BF16)
            yc[pl.ds(r0, size), :] = jnp.zeros((size, d), F32)

        _for_pieces(nch, sizes, pack)

    nch = nch_ref[0]

    def expert(r0, size):
        xb = xc[pl.ds(r0, size), :]
        g = _dot(xb, wg_ref[...])
        u = _dot(xb, wu_ref[...])
        a = (jax.nn.silu(g) * u).astype(BF16)
        yc[pl.ds(r0, size), :] += _dot(a, wd_ref[...])

    _for_pieces(nch, sizes, expert)

    @pl.when(f == n_f - 1)
    def _():
        lane = lax.broadcasted_iota(jnp.int32, (tm, LANES), 1)
        pick = lane == e
        gate = jnp.sum(jnp.where(pick, comb_ref[...], 0.0), axis=-1, keepdims=True)
        rk_col = jnp.sum(jnp.where(pick, rkc[...], 0.0), axis=-1, keepdims=True)

        def spread(r0, size):
            slot = lax.broadcasted_iota(jnp.int32, (tm, size), 1).astype(F32)
            onehot = jnp.where(rk_col - r0.astype(F32) == slot, 1.0, 0.0).astype(BF16)
            acc[...] += gate * _dot(onehot, yc[pl.ds(r0, size), :].astype(BF16))

        _for_pieces(nch, sizes, spread)

    @pl.when(jnp.logical_and(e == n_e - 1, f == n_f - 1))
    def _():
        _residual_out(x_ref, mod_ref, acc, fg_ref, o_ref, d, final_norm)


def moe_block(h2, x1, mod, comb, wg, wu, wd, final_g, *, tm, tf, rows_per_group, final_norm):
    n, d = x1.shape
    n_e, _, ff = wg.shape
    n_f = ff // tf
    assert ff % tf == 0 and n % tm == 0 and tm % LANES == 0
    sizes = tuple(s for s in (4 * LANES, 2 * LANES, LANES) if s <= tm)
    cap = tm
    tiles_per_group = rows_per_group // tm
    rmod = mod.shape[1]
    tri = jnp.asarray(np.tril(np.ones((tm, tm), np.float32), -1), dtype=BF16)
    kern = functools.partial(_moe_kernel, d=d, n_e=n_e, n_f=n_f, tm=tm, sizes=sizes, final_norm=final_norm)
    return pl.pallas_call(
        kern,
        grid=(n // tm, n_e, n_f),
        in_specs=[
            pl.BlockSpec((tm, d), lambda i, e, f: (i, 0)),
            pl.BlockSpec((tm, d), lambda i, e, f: (i, 0)),
            pl.BlockSpec((None, rmod, mod.shape[2]), lambda i, e, f: (i // tiles_per_group, 0, 0)),
            pl.BlockSpec((tm, LANES), lambda i, e, f: (i, 0)),
            pl.BlockSpec((tm, tm), lambda i, e, f: (0, 0)),
            pl.BlockSpec((None, d, tf), lambda i, e, f: (e, 0, f)),
            pl.BlockSpec((None, d, tf), lambda i, e, f: (e, 0, f)),
            pl.BlockSpec((None, tf, d), lambda i, e, f: (e, f, 0)),
            pl.BlockSpec((1, d), lambda i, e, f: (0, 0)),
        ],
        out_specs=pl.BlockSpec((tm, d), lambda i, e, f: (i, 0)),
        out_shape=jax.ShapeDtypeStruct((n, d), F32),
        scratch_shapes=[pltpu.VMEM((tm, d), F32), pltpu.VMEM((tm, LANES), F32), pltpu.VMEM((LANES, tm), F32),
                        pltpu.VMEM((cap, d), BF16), pltpu.VMEM((cap, d), F32), pltpu.SMEM((1,), jnp.int32)],
        compiler_params=_cparams(("arbitrary", "arbitrary", "arbitrary")),
        name="moe",
    )(h2, x1, mod, comb, tri, wg, wu, wd, final_g.reshape(1, d))


def _moe_wide_kernel(h_ref, comb_ref, tri_ref, wg_ref, wu_ref, wd_ref, o_ref,
                     rkc, rkr, xc, yc, nch_ref, *, d, n_f, tm, tr, sizes):
    e = pl.program_id(1)
    f = pl.program_id(2)
    ch = sizes[-1]

    @pl.when(jnp.logical_and(e == 0, f == 0))
    def _():
        o_ref[...] = jnp.zeros_like(o_ref)
        before = jnp.zeros((1, LANES), F32)
        for blk in range(tm // tr):
            rows = slice(blk * tr, (blk + 1) * tr)
            sel = comb_ref[rows, :] > 0.0
            ones = jnp.where(sel, 1.0, 0.0)
            rank = _dot(tri_ref[...], ones.astype(BF16)) + before
            rkc[rows, :] = jnp.where(sel, rank, -1.0)
            before = before + jnp.sum(ones, axis=0, keepdims=True)
        rkr[...] = rkc[...].T

    @pl.when(f == 0)
    def _():
        rk_row = rkr[pl.ds(e, 1), :]
        cnt = jnp.sum(jnp.where(rk_row >= 0.0, 1.0, 0.0)).astype(jnp.int32)
        nch = (cnt + (ch - 1)) // ch
        nch_ref[0] = nch

        def pack(r0, size):
            slot = lax.broadcasted_iota(jnp.int32, (size, tm), 0).astype(F32)
            onehot = jnp.where(rk_row - r0.astype(F32) == slot, 1.0, 0.0).astype(BF16)
            xc[pl.ds(r0, size), :] = _dot(onehot, h_ref[...]).astype(BF16)
            yc[pl.ds(r0, size), :] = jnp.zeros((size, d), F32)

        _for_pieces(nch, sizes, pack)

    nch = nch_ref[0]

    def expert(r0, size):
        xb = xc[pl.ds(r0, size), :]
        g = _dot(xb, wg_ref[...])
        u = _dot(xb, wu_ref[...])
        a = (jax.nn.silu(g) * u).astype(BF16)
        yc[pl.ds(r0, size), :] += _dot(a, wd_ref[...])

    _for_pieces(nch, sizes, expert)

    @pl.when(f == n_f - 1)
    def _():
        lane = lax.broadcasted_iota(jnp.int32, (tm, LANES), 1)
        pick = lane == e
        gate = jnp.sum(jnp.where(pick, comb_ref[...], 0.0), axis=-1, keepdims=True)
        rk_col = jnp.sum(jnp.where(pick, rkc[...], 0.0), axis=-1, keepdims=True)

        def spread(r0, size):
            slot = lax.broadcasted_iota(jnp.int32, (tm, size), 1).astype(F32)
            onehot = jnp.where(rk_col - r0.astype(F32) == slot, 1.0, 0.0).astype(BF16)
            o_ref[...] += gate * _dot(onehot, yc[pl.ds(r0, size), :].astype(BF16))

        _for_pieces(nch, sizes, spread)


def _residual_kernel(x_ref, y_ref, mod_ref, fg_ref, o_ref, *, d, final_norm):
    _residual_out(x_ref, mod_ref, y_ref, fg_ref, o_ref, d, final_norm)


def moe_block_wide(h2, x1, mod, comb, wg, wu, wd, final_g, *, tm, tr, tf, rows_per_group, final_norm):
    n, d = x1.shape
    n_e, _, ff = wg.shape
    n_f = ff // tf
    assert ff % tf == 0 and n % tm == 0 and tm % tr == 0 and rows_per_group % tr == 0
    sizes = tuple(s for s in (4 * LANES, 2 * LANES, LANES) if s <= tm)
    tri = jnp.asarray(np.tril(np.ones((tr, tr), np.float32), -1), dtype=BF16)
    once = pl.Buffered(1)
    kern = functools.partial(_moe_wide_kernel, d=d, n_f=n_f, tm=tm, tr=tr, sizes=sizes)
    y = pl.pallas_call(
        kern,
        grid=(n // tm, n_e, n_f),
        in_specs=[
            pl.BlockSpec((tm, d), lambda i, e, f: (i, 0), pipeline_mode=once),
            pl.BlockSpec((tm, LANES), lambda i, e, f: (i, 0), pipeline_mode=once),
            pl.BlockSpec((tr, tr), lambda i, e, f: (0, 0), pipeline_mode=once),
            pl.BlockSpec((None, d, tf), lambda i, e, f: (e, 0, f)),
            pl.BlockSpec((None, d, tf), lambda i, e, f: (e, 0, f)),
            pl.BlockSpec((None, tf, d), lambda i, e, f: (e, f, 0)),
        ],
        out_specs=pl.BlockSpec((tm, d), lambda i, e, f: (i, 0)),
        out_shape=jax.ShapeDtypeStruct((n, d), F32),
        scratch_shapes=[pltpu.VMEM((tm, LANES), F32), pltpu.VMEM((LANES, tm), F32),
                        pltpu.VMEM((tm, d), BF16), pltpu.VMEM((tm, d), F32), pltpu.SMEM((1,), jnp.int32)],
        compiler_params=_cparams(("arbitrary", "arbitrary", "arbitrary")),
        name="moe_wide",
    )(h2, comb, tri, wg, wu, wd)
    rmod = mod.shape[1]
    return pl.pallas_call(
        functools.partial(_residual_kernel, d=d, final_norm=final_norm),
        grid=(n // tr,),
        in_specs=[
            pl.BlockSpec((tr, d), lambda i: (i, 0)),
            pl.BlockSpec((tr, d), lambda i: (i, 0)),
            pl.BlockSpec((None, rmod, mod.shape[2]), lambda i: (i // (rows_per_group // tr), 0, 0)),
            pl.BlockSpec((1, d), lambda i: (0, 0)),
        ],
        out_specs=pl.BlockSpec((tr, d), lambda i: (i, 0)),
        out_shape=jax.ShapeDtypeStruct((n, d), F32),
        compiler_params=_cparams(("arbitrary",)),
        name="moe_residual",
    )(x1, y, mod, final_g.reshape(1, d))


def _pick_tf(ff, target):
    best = LANES
    for t in range(LANES, target + 1, LANES):
        if ff % t == 0:
            best = t
    return best


def kernel(x_prompt, x_sample, cache_k, cache_v, cache_logf, state_conv, page_table, c_prompt, c_sample, w_ada, b_ada, norm1_g, norm2_g, w_in, b_in, conv_w, out_g_attn, out_g_conv, w_o, ffn_w_gate, ffn_w_up, ffn_w_down, router_w, router_b, moe_w_gate, moe_w_up, moe_w_down, final_g):
    batch, seq, d = x_prompt.shape
    bd, ts, _ = x_sample.shape
    depth, n_pool, page, nh, dh = cache_k.shape
    assert (nh, dh) == (N_HEADS, HEAD_DIM)
    a_dim = nh * dh
    c_dim = conv_w.shape[-1]
    n_pages = page_table.shape[1]
    np_rows = batch * seq
    ns_rows = bd * ts
    rows_q = ts * nh

    tm_p = min(512, seq)
    tq = min(512, seq)
    tm_f = min(1024, seq)
    ppc = next(p for p in (32, 16, 8) if n_pages % p == 0)

    n_seq = batch + bd
    pad = -n_seq % SUBLANES
    c_all = jnp.concatenate([c_prompt, c_sample, jnp.zeros((pad, d), F32)], axis=0)
    mods = ada_modulation(c_all, w_ada, b_ada)

    xp = x_prompt.reshape(np_rows, d)
    xs = jnp.transpose(x_sample, (1, 0, 2)).reshape(ns_rows, d)
    k_t = jnp.transpose(cache_k, (0, 1, 3, 4, 2))
    v_t = jnp.transpose(cache_v, (0, 1, 3, 4, 2))
    sfx = page_suffix_sums(jnp.transpose(cache_logf, (0, 1, 3, 2)))
    conv_zero = jnp.zeros((batch, 2, c_dim), F32)
    rq = np.arange(rows_q)
    qmask = jnp.asarray((rq % nh)[:, None] == (np.arange(a_dim) // dh)[None, :])
    causal = jnp.asarray(np.arange(SUBLANES)[None, :] <= (rq // nh)[:, None])

    outs = {k: [] for k in ("lp", "cp", "ls", "cs")}
    kv_p = (jnp.zeros((depth * np_rows * nh, dh), F32),) * 2
    kv_s = (jnp.zeros((depth * ns_rows * nh, dh), F32),) * 2
    for l in range(depth):
        j = l // 2
        last = l == depth - 1
        mod_p = mods[l, 0:batch][:, None, :]
        mod_s = jnp.tile(mods[l, batch:batch + bd], (ts, 1))[None]
        moe = l % 2 == 1
        if moe:
            wg, wu, wd = moe_w_gate[j].astype(BF16), moe_w_up[j].astype(BF16), moe_w_down[j].astype(BF16)
            router = (router_w[j], router_b[j])
        else:
            wg, wu, wd = ffn_w_gate[j].astype(BF16), ffn_w_up[j].astype(BF16), ffn_w_down[j].astype(BF16)
            router = None
        tf = _pick_tf(wg.shape[-1], 1024 if moe else 1536)

        def ffn(res, mod, tm, rows_per_group):
            if moe and tm == tm_f and rows_per_group % (2 * tm) == 0:
                return moe_block_wide(res[1], res[0], mod, res[2], wg, wu, wd, final_g, tm=2 * tm, tr=tm, tf=tf,
                                      rows_per_group=rows_per_group, final_norm=last)
            if moe:
                return moe_block(res[1], res[0], mod, res[2], wg, wu, wd, final_g, tm=tm, tf=tf,
                                 rows_per_group=rows_per_group, final_norm=last)
            return ffn_block(res[1], res[0], mod, wg, wu, wd, final_g, tm=tm, tf=tf,
                             rows_per_group=rows_per_group, final_norm=last)

        qb, kb, vb, k1, v1, lf1, negc, mconv, cst = in_projection(
            xp, mod_p, conv_zero, norm1_g[l], w_in[l], b_in[l], conv_w[l], out_g_conv[l],
            tm=tm_p, stride=1, rows_per_group=seq, logit_scale=LOG2E, layer=l, depth=depth, kv_buf=kv_p)
        kv_p = (k1, v1)
        m_attn = prompt_attention(qb, kb, vb, negc, out_g_attn[l], batch=batch, seq=seq, tq=tq, n_pairs=4)
        res = out_projection(m_attn, mconv, xp, mod_p, w_o[l], norm2_g[l], router,
                             tm=tm_p, rows_per_group=seq)
        xp = ffn(res, mod_p, tm_f if moe else tm_p, seq)
        outs["lp"].append(lf1.reshape(batch, seq, nh))
        tiles_per_seq = seq // tm_p
        outs["cp"].append(cst[tiles_per_seq - 1::tiles_per_seq])

        prev_s = jnp.transpose(state_conv[l], (1, 0, 2)).reshape(1, 2 * bd, c_dim)
        qb, kb, vb, k2, v2, lf2, negc, mconv, cst = in_projection(
            xs, mod_s, prev_s, norm1_g[l], w_in[l], b_in[l], conv_w[l], out_g_conv[l],
            tm=ns_rows, stride=bd, rows_per_group=ns_rows, layer=l, depth=depth, kv_buf=kv_s)
        kv_s = (k2, v2)
        to_seq = lambda a: jnp.transpose(a.reshape(ts, bd, -1), (1, 0, 2))
        qbig = jnp.where(qmask[None], jnp.repeat(to_seq(qb), nh, axis=1), jnp.zeros((), BF16))
        rows_pad = ((0, 0), (0, SUBLANES - ts), (0, 0))
        knew = jnp.pad(to_seq(kb), rows_pad)
        vnew = jnp.pad(to_seq(vb), rows_pad)
        nc = jnp.transpose(negc[0].reshape(nh, ts, bd), (2, 0, 1))
        nc = jnp.pad(nc, ((0, 0), (0, 0), (0, SUBLANES - ts)))
        bias_new = jnp.where(causal[None], jnp.tile(nc, (1, ts, 1)), NEG)
        pt_flat = page_table.reshape(-1).astype(jnp.int32)
        a_s = sample_attention(pt_flat, qbig, knew, vnew, bias_new, out_g_attn[l],
                               k_t, v_t, sfx, l, n_pages=n_pages, ppc=ppc)
        m_attn_s = jnp.transpose(a_s, (1, 0, 2)).reshape(ns_rows, a_dim)
        res = out_projection(m_attn_s, mconv, xs, mod_s, w_o[l], norm2_g[l], router,
                             tm=ns_rows, rows_per_group=ns_rows)
        xs = ffn(res, mod_s, ns_rows, ns_rows)
        outs["ls"].append(to_seq(lf2))
        outs["cs"].append(jnp.transpose(cst.reshape(2, bd, c_dim), (1, 0, 2)))

    y_prompt = xp.reshape(batch, seq, d)
    y_sample = jnp.transpose(xs.reshape(ts, bd, d), (1, 0, 2))
    st = lambda key: jnp.stack(outs[key])
    kv_prompt = lambda a: a.reshape(depth, batch, seq, nh, dh)
    kv_sample = lambda a: jnp.transpose(a.reshape(depth, ts, bd, nh, dh), (0, 2, 1, 3, 4))
    return (y_prompt, y_sample, kv_prompt(kv_p[0]), kv_prompt(kv_p[1]), st("lp"), st("cp"),
            kv_sample(kv_s[0]), kv_sample(kv_s[1]), st("ls"), st("cs"))
```
